```python
import math
import jax, jax.numpy as jnp
from jax import lax
import numpy as np

D_MODEL = 2048
BATCH = 8
SEQ = 4096
DEPTH = 4

GRID_W = 64
CTX_LEN = 256
SSD_HEAD_DIM = 64
SSD_WIDTH = D_MODEL
SSD_HEADS = SSD_WIDTH // SSD_HEAD_DIM
SSD_GROUPS = 4
SSD_STATE = 128
SSD_CHUNK = 128
SSD_XBC = SSD_WIDTH + 2 * SSD_GROUPS * SSD_STATE
SHORT_CONV = 4
SHORT_PAD = (2, 1)
HEAD_DIM = 128
ATTN_HEADS = D_MODEL // HEAD_DIM
ATTN_KV = ATTN_HEADS // 4
ATTN_GROUP = ATTN_HEADS // ATTN_KV
WINDOW = 128
ATTN_BLOCK = 128
ATTN_SCALE = HEAD_DIM ** -0.5
ROPE_BASE = 10000.0
CONF_WIDTH = D_MODEL
CONF_KERNEL = 31
CONF_PAD = (15, 15)
RG_WIDTH = D_MODEL
RG_BLOCK = 128
RG_BLOCKS = RG_WIDTH // RG_BLOCK
RG_C = 8.0
DENSE_FF = ((8 * D_MODEL // 3 + 255) // 256) * 256
N_EXPERTS = 8
TOP_K = 2
MOE_FF = 2 * D_MODEL
ALPHA = (2 * DEPTH) ** 0.25
BETA = (8 * DEPTH) ** -0.25
LN_EPS = 1e-5
N_EVEN = (DEPTH + 1) // 2
N_ODD = DEPTH // 2
AB_SIZES = (SSD_WIDTH, SSD_XBC, 2 * SSD_HEADS, ATTN_HEADS * HEAD_DIM, ATTN_KV * HEAD_DIM, ATTN_KV * HEAD_DIM)
AB_IN = sum(AB_SIZES)
AB_SPLITS = tuple(int(v) for v in np.cumsum(AB_SIZES)[:-1])
AB_OUT = SSD_WIDTH + ATTN_HEADS * HEAD_DIM
CD_SIZES = (2 * CONF_WIDTH, RG_WIDTH, RG_WIDTH)
CD_IN = sum(CD_SIZES)
CD_SPLITS = tuple(int(v) for v in np.cumsum(CD_SIZES)[:-1])
CD_OUT = CONF_WIDTH + RG_WIDTH

kernel_name = 'hybrid_ssd_swa_conformer_rglru_moe_dit'

F32 = jnp.float32


def layer_norm(x, g, b):
    xf = x.astype(F32)
    mu = xf.mean(-1, keepdims=True)
    var = jnp.square(xf - mu).mean(-1, keepdims=True)
    return ((xf - mu) * lax.rsqrt(var + LN_EPS)).astype(x.dtype) * g + b


def rms_norm(x, g):
    xf = x.astype(F32)
    return (xf * lax.rsqrt(jnp.mean(xf * xf, -1, keepdims=True) + LN_EPS)).astype(x.dtype) * g


def dwconv(x, w, b, pad):
    ch = x.shape[-1]
    y = lax.conv_general_dilated(x, w.astype(x.dtype)[:, None, :], window_strides=(1,), padding=(pad,),
                                 dimension_numbers=('NWC', 'WIO', 'NWC'), feature_group_count=ch)
    return y + b


def rope_1d(x, pos):
    half = x.shape[-1] // 2
    inv_freq = ROPE_BASE ** (-jnp.arange(half, dtype=F32) / half)
    ang = pos.astype(F32)[:, None] * inv_freq[None, :]
    cos = jnp.cos(ang)[:, None, :].astype(x.dtype)
    sin = jnp.sin(ang)[:, None, :].astype(x.dtype)
    x1, x2 = x[..., :half], x[..., half:]
    return jnp.concatenate([x1 * cos - x2 * sin, x2 * cos + x1 * sin], axis=-1)


def rope_2d(x, rows, cols):
    h = x.shape[-1] // 2
    return jnp.concatenate([rope_1d(x[..., :h], rows), rope_1d(x[..., h:], cols)], axis=-1)


def ssd_scan(xs, dA, bm, cm, h0):
    n, L, H, P = xs.shape
    G, N = bm.shape[2], bm.shape[3]
    R = H // G
    Q = SSD_CHUNK
    nc = L // Q
    X = xs.astype(F32).reshape(n, nc, Q, G, R, P)
    A = dA.astype(F32).reshape(n, nc, Q, G, R)
    Bc = bm.astype(F32).reshape(n, nc, Q, G, N)
    Cc = cm.astype(F32).reshape(n, nc, Q, G, N)
    A_cs = jnp.cumsum(A, axis=2)
    lower = jnp.tril(jnp.ones((Q, Q), dtype=bool))
    seg = A_cs[:, :, :, None] - A_cs[:, :, None, :]
    decay = jnp.exp(jnp.where(lower[None, None, :, :, None, None], seg, -jnp.inf))
    scores = jnp.einsum('bclgn,bcsgn->bclsg', Cc, Bc)[..., None] * decay
    y_diag = jnp.einsum('bclsgr,bcsgrp->bclgrp', scores, X)
    to_end = jnp.exp(A_cs[:, :, -1:] - A_cs)
    chunk_states = jnp.einsum('bclgn,bclgrp->bcgrpn', Bc, X * to_end[..., None])
    chunk_decay = jnp.exp(A_cs[:, :, -1])

    def step(h, inp):
        s_c, d_c = inp
        return h * d_c[..., None, None] + s_c, h

    h_last, h_in = lax.scan(step, h0.astype(F32).reshape(n, G, R, P, N),
                            (jnp.moveaxis(chunk_states, 1, 0), jnp.moveaxis(chunk_decay, 1, 0)))
    h_in = jnp.moveaxis(h_in, 0, 1)
    y_off = jnp.einsum('bclgn,bcgrpn->bclgrp', Cc, h_in) * jnp.exp(A_cs)[..., None]
    return (y_diag + y_off).reshape(n, L, H, P), h_last.reshape(n, H, P, N)


def ssd_direction(xs, bm, cm, dt, a_neg_d, d, h0):
    dtd = dt[:, :, d]
    flip = (lambda t: jnp.flip(t, axis=1)) if d == 1 else (lambda t: t)
    y, h_last = ssd_scan(flip(xs.astype(F32) * dtd[..., None]), flip(dtd * a_neg_d), flip(bm), flip(cm), h0)
    return flip(y), h_last


def linear_scan(a, b, h0):
    b = b.at[:, 0].add(a[:, 0] * h0)

    def combine(left, right):
        a_l, b_l = left
        a_r, b_r = right
        return a_l * a_r, a_r * b_l + b_r

    _, h = lax.associative_scan(combine, (a, b), axis=1)
    return h


def window_attention(q, k, v, kc, vc, sink):
    n, S = q.shape[0], q.shape[1]
    nb = S // ATTN_BLOCK
    qb = q.reshape(n, nb, ATTN_BLOCK, ATTN_KV, ATTN_GROUP, HEAD_DIM)
    pad = ((0, 0), (ATTN_BLOCK, ATTN_BLOCK), (0, 0), (0, 0))
    kp = jnp.pad(k, pad).reshape(n, nb + 2, ATTN_BLOCK, ATTN_KV, HEAD_DIM)
    vp = jnp.pad(v, pad).reshape(n, nb + 2, ATTN_BLOCK, ATTN_KV, HEAD_DIM)
    kb = jnp.concatenate([kp[:, :-2], kp[:, 1:-1], kp[:, 2:]], axis=2)
    vb = jnp.concatenate([vp[:, :-2], vp[:, 1:-1], vp[:, 2:]], axis=2)
    s_loc = jnp.einsum('bnqhgd,bnkhd->bnhgqk', qb, kb).astype(F32) * ATTN_SCALE
    blk = jnp.arange(nb)[:, None] * ATTN_BLOCK
    qpos = blk + jnp.arange(ATTN_BLOCK)[None, :]
    kpos = blk - ATTN_BLOCK + jnp.arange(3 * ATTN_BLOCK)[None, :]
    valid = ((jnp.abs(qpos[:, :, None] - kpos[:, None, :]) <= WINDOW)
             & (kpos[:, None, :] >= 0) & (kpos[:, None, :] < S))
    s_loc = jnp.where(valid[None, :, None, None], s_loc, -jnp.inf)
    s_ctx = jnp.einsum('bnqhgd,bchd->bnhgqc', qb, kc).astype(F32) * ATTN_SCALE
    sk = jnp.broadcast_to(sink.astype(F32).reshape(ATTN_KV, ATTN_GROUP)[None, None, :, :, None, None],
                          s_loc.shape[:-1] + (1,))
    p = jax.nn.softmax(jnp.concatenate([s_loc, s_ctx, sk], axis=-1), axis=-1).astype(v.dtype)
    n_loc = 3 * ATTN_BLOCK
    n_ctx = kc.shape[1]
    o = (jnp.einsum('bnhgqk,bnkhd->bnqhgd', p[..., :n_loc], vb)
         + jnp.einsum('bnhgqc,bchd->bnqhgd', p[..., n_loc:n_loc + n_ctx], vc))
    return o.reshape(n, S, ATTN_HEADS * HEAD_DIM)


def context_attention(q, k, v, sink):
    n, L = q.shape[0], q.shape[1]
    qg = q.reshape(n, L, ATTN_KV, ATTN_GROUP, HEAD_DIM)
    s = jnp.einsum('bqhgd,bkhd->bhgqk', qg, k).astype(F32) * ATTN_SCALE
    sk = jnp.broadcast_to(sink.astype(F32).reshape(ATTN_KV, ATTN_GROUP)[None, :, :, None, None], s.shape[:-1] + (1,))
    p = jax.nn.softmax(jnp.concatenate([s, sk], axis=-1), axis=-1)[..., :-1].astype(v.dtype)
    o = jnp.einsum('bhgqk,bkhd->bqhgd', p, v)
    return o.reshape(n, L, ATTN_HEADS * HEAD_DIM)


def even_mixer(hc, hx, w_in, w_out, conv_w, conv_b, dt_bias, a_log, d_skip, norm_w, sink, rows, cols, need_ctx):
    n = hx.shape[0]
    z_c, xbc_c, dt_c, q_c, k_c, v_c = jnp.split(hc @ w_in, AB_SPLITS, axis=-1)
    z_x, xbc_x, dt_x, q_x, k_x, v_x = jnp.split(hx @ w_in, AB_SPLITS, axis=-1)
    a_neg = -jnp.exp(a_log.astype(F32))

    def ssd_prep(xbc, dt_raw):
        L = xbc.shape[1]
        xbc = jax.nn.silu(dwconv(xbc, conv_w, conv_b, SHORT_PAD))
        xs, bm, cm = jnp.split(xbc, (SSD_WIDTH, SSD_WIDTH + SSD_GROUPS * SSD_STATE), axis=-1)
        dt = jax.nn.softplus(dt_raw.astype(F32).reshape(n, L, 2, SSD_HEADS) + dt_bias.astype(F32))
        return (xs.reshape(n, L, SSD_HEADS, SSD_HEAD_DIM), bm.reshape(n, L, SSD_GROUPS, SSD_STATE),
                cm.reshape(n, L, SSD_GROUPS, SSD_STATE), dt)

    pc = ssd_prep(xbc_c, dt_c)
    px = ssd_prep(xbc_x, dt_x)
    dsk = d_skip.astype(F32)[:, None]
    y_c = pc[0].astype(F32) * dsk
    y_x = px[0].astype(F32) * dsk
    h_zero = jnp.zeros((n, SSD_HEADS, SSD_HEAD_DIM, SSD_STATE), F32)
    for d in range(2):
        yd_c, h_ctx = ssd_direction(*pc, a_neg[d], d, h_zero)
        yd_x, _ = ssd_direction(*px, a_neg[d], d, h_ctx)
        y_c = y_c + yd_c
        y_x = y_x + yd_x

    def ssd_out(y, z):
        L = y.shape[1]
        return rms_norm(y.reshape(n, L, SSD_WIDTH).astype(z.dtype) * jax.nn.silu(z), norm_w)

    def heads(t, h):
        return t.reshape(t.shape[0], t.shape[1], h, HEAD_DIM)

    kc = heads(k_c, ATTN_KV)
    vc = heads(v_c, ATTN_KV)
    qx = rope_2d(heads(q_x, ATTN_HEADS), rows, cols)
    kx = rope_2d(heads(k_x, ATTN_KV), rows, cols)
    att_x = window_attention(qx, kx, heads(v_x, ATTN_KV), kc, vc, sink)
    out_x = jnp.concatenate([ssd_out(y_x, z_x), att_x], axis=-1) @ w_out
    out_c = None
    if need_ctx:
        att_c = context_attention(heads(q_c, ATTN_HEADS), kc, vc, sink)
        out_c = jnp.concatenate([ssd_out(y_c, z_c), att_c], axis=-1) @ w_out
    return out_c, out_x


def odd_mixer(hc, hx, w_in, w_out, dw_w, dw_b, cln_g, cln_b, rconv_w, rconv_b, gate_w, gate_b, lam, need_ctx):
    n = hx.shape[0]
    glu_x, rx_x, rgate_x = jnp.split(hx @ w_in, CD_SPLITS, axis=-1)
    if need_ctx:
        glu_c, rx_c, rgate_c = jnp.split(hc @ w_in, CD_SPLITS, axis=-1)
    else:
        rx_c = hc @ w_in[:, 2 * CONF_WIDTH:2 * CONF_WIDTH + RG_WIDTH]

    def conformer(glu):
        u = glu[..., :CONF_WIDTH] * jax.nn.sigmoid(glu[..., CONF_WIDTH:])
        u = dwconv(u, dw_w, dw_b, CONF_PAD)
        return jax.nn.silu(layer_norm(u, cln_g, cln_b))

    def rg_coeffs(xr, d):
        L = xr.shape[1]
        xb = xr.reshape(n, L, RG_BLOCKS, RG_BLOCK)
        pre = jnp.einsum('blhi,ghij->gblhj', xb, gate_w[d]).reshape(2, n, L, RG_WIDTH) + gate_b[d][:, None, None, :]
        r = jax.nn.sigmoid(pre[0].astype(F32))
        i = jax.nn.sigmoid(pre[1].astype(F32))
        log_a = -RG_C * r * jax.nn.softplus(-lam[d].astype(F32))
        a = jnp.exp(log_a)
        b = jnp.sqrt(-jnp.expm1(2.0 * log_a)) * i * xr.astype(F32)
        return a, b

    xr_c = dwconv(rx_c, rconv_w, rconv_b, SHORT_PAD)
    xr_x = dwconv(rx_x, rconv_w, rconv_b, SHORT_PAD)
    h_zero = jnp.zeros((n, RG_WIDTH), F32)
    hsum_c = jnp.zeros(xr_c.shape, F32)
    hsum_x = jnp.zeros(xr_x.shape, F32)
    for d in range(2):
        flip = (lambda t: jnp.flip(t, axis=1)) if d == 1 else (lambda t: t)
        a_c, b_c = rg_coeffs(xr_c, d)
        hs_c = linear_scan(flip(a_c), flip(b_c), h_zero)
        a_x, b_x = rg_coeffs(xr_x, d)
        hs_x = linear_scan(flip(a_x), flip(b_x), hs_c[:, -1])
        hsum_x = hsum_x + flip(hs_x)
        if need_ctx:
            hsum_c = hsum_c + flip(hs_c)
    rg_out_x = hsum_x.astype(hx.dtype) * jax.nn.gelu(rgate_x)
    out_x = jnp.concatenate([conformer(glu_x), rg_out_x], axis=-1) @ w_out
    out_c = None
    if need_ctx:
        rg_out_c = hsum_c.astype(hc.dtype) * jax.nn.gelu(rgate_c)
        out_c = jnp.concatenate([conformer(glu_c), rg_out_c], axis=-1) @ w_out
    return out_c, out_x


def swiglu(h, w1, w3, w2):
    return (jax.nn.silu(h @ w1) * (h @ w3)) @ w2


def moe_ffn(h, w_router, w1, w3, w2):
    logits = (h @ w_router).astype(F32)
    top_v, top_i = lax.top_k(logits, TOP_K)
    gates = jax.nn.softmax(top_v, axis=-1)
    combine = jnp.sum(jax.nn.one_hot(top_i, N_EXPERTS, dtype=F32) * gates[..., None], axis=-2).astype(h.dtype)
    out = jnp.zeros(h.shape, h.dtype)
    for e in range(N_EXPERTS):
        out = out + combine[..., e:e + 1] * swiglu(h, w1[e], w3[e], w2[e])
    return out


def setup_inputs(seed: int = 0) -> dict:
    key = jax.random.key(seed)
    keys = iter(jax.random.split(key, 48))

    def nrm(shape, scale):
        return jax.random.normal(next(keys), shape, F32) * scale

    def unif(shape, lo, hi):
        return jax.random.uniform(next(keys), shape, F32, lo, hi)

    D = D_MODEL
    dt0 = jnp.exp(unif((N_EVEN, 2, SSD_HEADS), math.log(1e-3), math.log(1e-1)))
    lam_a = unif((N_ODD, 2, RG_WIDTH), 0.9, 0.999) ** (1.0 / RG_C)
    return {
        'x': nrm((BATCH, SEQ, D), 1.0),
        'c': nrm((BATCH, D), 1.0),
        'ctx': nrm((BATCH, CTX_LEN, D), 1.0),
        'c_ctx': nrm((D,), 1.0),
        'mod_w': nrm((DEPTH, D, 6 * D), 0.5 * D ** -0.5),
        'mod_b': nrm((DEPTH, 6 * D), 0.02),
        'ln_g': 1.0 + nrm((DEPTH, 2, D), 0.02),
        'ln_b': nrm((DEPTH, 2, D), 0.02),
        'ab_w_in': nrm((N_EVEN, D, AB_IN), D ** -0.5),
        'ab_w_out': nrm((N_EVEN, AB_OUT, D), BETA * AB_OUT ** -0.5),
        'ssd_conv_w': nrm((N_EVEN, SHORT_CONV, SSD_XBC), SHORT_CONV ** -0.5),
        'ssd_conv_b': nrm((N_EVEN, SSD_XBC), 0.02),
        'ssd_dt_bias': dt0 + jnp.log(-jnp.expm1(-dt0)),
        'ssd_a_log': jnp.log(unif((N_EVEN, 2, SSD_HEADS), 1.0, 16.0)),
        'ssd_d': 1.0 + nrm((N_EVEN, SSD_HEADS), 0.1),
        'ssd_norm_w': 1.0 + nrm((N_EVEN, SSD_WIDTH), 0.02),
        'attn_sink': nrm((N_EVEN, ATTN_HEADS), 0.5),
        'ffn_w1': nrm((N_EVEN, D, DENSE_FF), D ** -0.5),
        'ffn_w3': nrm((N_EVEN, D, DENSE_FF), D ** -0.5),
        'ffn_w2': nrm((N_EVEN, DENSE_FF, D), BETA * DENSE_FF ** -0.5),
        'cd_w_in': nrm((N_ODD, D, CD_IN), D ** -0.5),
        'cd_w_out': nrm((N_ODD, CD_OUT, D), BETA * CD_OUT ** -0.5),
        'conf_dw_w': nrm((N_ODD, CONF_KERNEL, CONF_WIDTH), CONF_KERNEL ** -0.5),
        'conf_dw_b': nrm((N_ODD, CONF_WIDTH), 0.02),
        'conf_ln_g': 1.0 + nrm((N_ODD, CONF_WIDTH), 0.02),
        'conf_ln_b': nrm((N_ODD, CONF_WIDTH), 0.02),
        'rg_conv_w': nrm((N_ODD, SHORT_CONV, RG_WIDTH), SHORT_CONV ** -0.5),
        'rg_conv_b': nrm((N_ODD, RG_WIDTH), 0.02),
        'rg_gate_w': nrm((N_ODD, 2, 2, RG_BLOCKS, RG_BLOCK, RG_BLOCK), RG_BLOCK ** -0.5),
        'rg_gate_b': nrm((N_ODD, 2, 2, RG_WIDTH), 0.02),
        'rg_lambda': jnp.log(lam_a) - jnp.log1p(-lam_a),
        'moe_router': nrm((N_ODD, D, N_EXPERTS), D ** -0.5),
        'moe_w1': nrm((N_ODD, N_EXPERTS, D, MOE_FF), D ** -0.5),
        'moe_w3': nrm((N_ODD, N_EXPERTS, D, MOE_FF), D ** -0.5),
        'moe_w2': nrm((N_ODD, N_EXPERTS, MOE_FF, D), BETA * MOE_FF ** -0.5),
    }


def reference(x, c, ctx, c_ctx, mod_w, mod_b, ln_g, ln_b, ab_w_in, ab_w_out, ssd_conv_w, ssd_conv_b,
              ssd_dt_bias, ssd_a_log, ssd_d, ssd_norm_w, attn_sink, ffn_w1, ffn_w3, ffn_w2, cd_w_in, cd_w_out,
              conf_dw_w, conf_dw_b, conf_ln_g, conf_ln_b, rg_conv_w, rg_conv_b, rg_gate_w, rg_gate_b, rg_lambda,
              moe_router, moe_w1, moe_w3, moe_w2):
    S = x.shape[1]
    n_ctx = ctx.shape[1]
    ROWS = S // GRID_W
    rows = jnp.repeat(jnp.arange(ROWS), GRID_W)
    cols = jnp.tile(jnp.arange(GRID_W), ROWS)
    sc = jax.nn.silu(c)
    scc = jax.nn.silu(c_ctx)
    for l in range(DEPTH):
        need_ctx = l < DEPTH - 1
        j = l // 2
        mod = sc @ mod_w[l] + mod_b[l]
        modc = scc @ mod_w[l] + mod_b[l]
        shm, scm, gm, shf, scf, gf = jnp.split(mod[:, None, :], 6, axis=-1)
        shmc, scmc, gmc, shfc, scfc, gfc = jnp.split(modc, 6)
        hx = x * (1.0 + scm) + shm
        hc = ctx * (1.0 + scmc) + shmc
        if l % 2 == 0:
            mc, mx = even_mixer(hc, hx, ab_w_in[j], ab_w_out[j], ssd_conv_w[j], ssd_conv_b[j], ssd_dt_bias[j],
                                ssd_a_log[j], ssd_d[j], ssd_norm_w[j], attn_sink[j], rows, cols, need_ctx)
        else:
            mc, mx = odd_mixer(hc, hx, cd_w_in[j], cd_w_out[j], conf_dw_w[j], conf_dw_b[j], conf_ln_g[j],
                               conf_ln_b[j], rg_conv_w[j], rg_conv_b[j], rg_gate_w[j], rg_gate_b[j], rg_lambda[j],
                               need_ctx)
        x = layer_norm(ALPHA * x + gm * mx, ln_g[l, 0], ln_b[l, 0])
        hx = x * (1.0 + scf) + shf
        if need_ctx:
            ctx = layer_norm(ALPHA * ctx + gmc * mc, ln_g[l, 0], ln_b[l, 0])
            h = jnp.concatenate([ctx * (1.0 + scfc) + shfc, hx], axis=1)
        else:
            h = hx
        if l % 2 == 0:
            f = swiglu(h, ffn_w1[j], ffn_w3[j], ffn_w2[j])
        else:
            f = moe_ffn(h, moe_router[j], moe_w1[j], moe_w3[j], moe_w2[j])
        if need_ctx:
            ctx = layer_norm(ALPHA * ctx + gfc * f[:, :n_ctx], ln_g[l, 1], ln_b[l, 1])
            fx = f[:, n_ctx:]
        else:
            fx = f
        x = layer_norm(ALPHA * x + gf * fx, ln_g[l, 1], ln_b[l, 1])
    return x
```

```python
import functools
import math

import jax
import jax.numpy as jnp
import numpy as np
from jax import lax
from jax.experimental import pallas as pl
from jax.experimental.pallas import tpu as pltpu

F32 = jnp.float32
BF16 = jnp.bfloat16
HIGHEST = lax.Precision.HIGHEST

LANE = 128
SUBLANE = 8
VMEM_BYTES = 64 * 1024 * 1024
VMEM_LIMIT = VMEM_BYTES * 7 // 8
MATMUL_VMEM_BUDGET = VMEM_BYTES * 5 // 8

D = 2048
GRID_W = 64
SSD_HEAD_DIM = 64
SSD_HEADS = D // SSD_HEAD_DIM
SSD_GROUPS = 4
SSD_STATE = 128
SSD_GROUP_HEADS = SSD_HEADS // SSD_GROUPS
SHORT_CONV = 4
SHORT_PAD_L = 2
HEAD_DIM = 128
ATTN_HEADS = D // HEAD_DIM
ATTN_KV = ATTN_HEADS // 4
ATTN_GROUP = ATTN_HEADS // ATTN_KV
WINDOW = 128
ATTN_SCALE = HEAD_DIM ** -0.5
ROPE_BASE = 10000.0
CONF_KERNEL = 31
CONF_PAD_L = 15
RG_BLOCK = 128
RG_BLOCKS = D // RG_BLOCK
RG_C = 8.0
N_EXPERTS = 8
LN_EPS = 1e-5

TT = 128
HALO = 16

AB_XS, AB_Z, AB_Q, AB_BC, AB_K, AB_V, AB_DT = 0, 2048, 4096, 6144, 7168, 7680, 8192
AB_N = 8448
CD_GLU_A, CD_GLU_B, CD_RX, CD_RGATE = 0, 2048, 4096, 6144
CD_N = 8192


def _cparams(*sem):
    return pltpu.CompilerParams(dimension_semantics=sem, vmem_limit_bytes=VMEM_LIMIT)


def _sigmoid(x):
    return 1.0 / (1.0 + jnp.exp(-x))


def _silu(x):
    return x * _sigmoid(x)


def _softplus(x):
    return jnp.maximum(x, 0.0) + jnp.log1p(jnp.exp(-jnp.abs(x)))


def _expm1(x):
    u = jnp.exp(x)
    near = jnp.where(u == 1.0, x, (u - 1.0) * x / jnp.log(u))
    return jnp.where(jnp.abs(x) < 0.5, near, u - 1.0)


def _gelu_tanh(x):
    return 0.5 * x * (1.0 + jnp.tanh(math.sqrt(2.0 / math.pi) * (x + 0.044715 * (x * x * x))))


def _mod_kernel(c_ref, w_ref, b_ref, o_ref):
    s = _silu(c_ref[...])
    o_ref[0] = jnp.dot(s, w_ref[0], preferred_element_type=F32, precision=HIGHEST) + b_ref[0]


def _modulation_tables(c, c_ctx, mod_w, mod_b):
    depth, _, n6 = mod_w.shape
    nb = c.shape[0]
    rows = -(-(nb + 1) // SUBLANE) * SUBLANE
    cc = jnp.zeros((rows, D), F32).at[:nb].set(c).at[nb].set(c_ctx)
    tn = 1024
    out = pl.pallas_call(
        _mod_kernel,
        grid=(depth, n6 // tn),
        in_specs=[pl.BlockSpec((rows, D), lambda l, j: (0, 0)),
                  pl.BlockSpec((1, D, tn), lambda l, j: (l, 0, j)),
                  pl.BlockSpec((1, 1, tn), lambda l, j: (l, 0, j))],
        out_specs=pl.BlockSpec((1, rows, tn), lambda l, j: (l, 0, j)),
        out_shape=jax.ShapeDtypeStruct((depth, rows, n6), F32),
        compiler_params=_cparams("arbitrary", "arbitrary"),
        name="modulation",
    )(cc, mod_w, mod_b.reshape(depth, 1, n6))
    lat = out[:, :nb].reshape(depth, nb, 1, 6, D)
    ctx = jnp.broadcast_to(out[:, nb].reshape(depth, 1, 1, 6, D), (depth, nb, 1, 6, D))
    return jnp.concatenate([ctx, lat], axis=2)


def _post_kernel(*refs, alpha, gate_row, mod_rows, has_res, has_h, has_router):
    it = iter(refs)
    x_ref = next(it)
    if has_res:
        m_ref, modg_ref, ln_ref = next(it), next(it), next(it)
    if has_h:
        modn_ref = next(it)
    if has_router:
        wr_ref = next(it)
    if has_res:
        xo_ref = next(it)
    if has_h:
        h_ref = next(it)
    if has_router:
        comb_ref = next(it)

    x = x_ref[0]
    if has_res:
        g = modg_ref[0, 0, gate_row:gate_row + 1, :]
        xf = alpha * x + g * m_ref[0]
        mu = jnp.mean(xf, axis=-1, keepdims=True)
        xc = xf - mu
        var = jnp.mean(xc * xc, axis=-1, keepdims=True)
        x = xc * lax.rsqrt(var + LN_EPS) * ln_ref[0:1, :] + ln_ref[1:2, :]
        xo_ref[0] = x
    if has_h:
        sh = modn_ref[0, 0, mod_rows[0]:mod_rows[0] + 1, :]
        sc = modn_ref[0, 0, mod_rows[1]:mod_rows[1] + 1, :]
        h = x * (1.0 + sc) + sh
        h_ref[0] = h.astype(BF16)
        if has_router:
            logits = jnp.dot(h, wr_ref[...], preferred_element_type=F32, precision=HIGHEST)
            lane = lax.broadcasted_iota(jnp.int32, logits.shape, 1).astype(F32)
            neg = jnp.float32(-jnp.inf)
            lg = jnp.where(lane < N_EXPERTS, logits, neg)
            m1 = jnp.max(lg, axis=-1, keepdims=True)
            i1 = jnp.min(jnp.where(lg == m1, lane, float(LANE)), axis=-1, keepdims=True)
            lg2 = jnp.where(lane == i1, neg, lg)
            m2 = jnp.max(lg2, axis=-1, keepdims=True)
            i2 = jnp.min(jnp.where(lg2 == m2, lane, float(LANE)), axis=-1, keepdims=True)
            e2 = jnp.exp(m2 - m1)
            den = 1.0 + e2
            comb_ref[0] = jnp.where(lane == i1, 1.0 / den, 0.0) + jnp.where(lane == i2, e2 / den, 0.0)


def _post(x, n_ctx, *, res=None, nxt=None, router=None):
    nb, tt, _ = x.shape
    tm = 256 if n_ctx % 256 == 0 else TT
    nct = n_ctx // tm
    tok = pl.BlockSpec((1, tm, D), lambda b, i: (b, i, 0))
    modspec = pl.BlockSpec((1, 1, 6, D), lambda b, i: (b, jnp.where(i >= nct, 1, 0), 0, 0))
    args, in_specs, out_shape, out_specs = [x], [tok], [], []
    if res is not None:
        m, modt, gate_row, ln_gb, alpha = res
        args += [m, modt, ln_gb]
        in_specs += [tok, modspec, pl.BlockSpec((2, D), lambda b, i: (0, 0))]
        out_shape.append(jax.ShapeDtypeStruct((nb, tt, D), F32))
        out_specs.append(tok)
    else:
        gate_row, alpha = 0, 1.0
    if nxt is not None:
        args.append(nxt[0])
        in_specs.append(modspec)
        out_shape.append(jax.ShapeDtypeStruct((nb, tt, D), BF16))
        out_specs.append(tok)
    if router is not None:
        args.append(router)
        in_specs.append(pl.BlockSpec((D, LANE), lambda b, i: (0, 0)))
        out_shape.append(jax.ShapeDtypeStruct((nb, tt, LANE), F32))
        out_specs.append(pl.BlockSpec((1, tm, LANE), lambda b, i: (b, i, 0)))
    kern = functools.partial(_post_kernel, alpha=alpha, gate_row=gate_row,
                             mod_rows=(nxt[1], nxt[2]) if nxt is not None else (0, 0),
                             has_res=res is not None, has_h=nxt is not None, has_router=router is not None)
    return pl.pallas_call(kern, grid=(nb, tt // tm), in_specs=in_specs, out_specs=out_specs, out_shape=out_shape,
                          compiler_params=_cparams("parallel", "arbitrary"), name="post")(*args)


def _mm_kernel(*refs, n_a):
    o_ref = refs[-1]
    acc = jnp.dot(refs[0][...], refs[n_a][...], preferred_element_type=F32)
    for t in range(1, n_a):
        acc = acc + jnp.dot(refs[t][...], refs[n_a + t][...], preferred_element_type=F32)
    o_ref[...] = acc.astype(o_ref.dtype)


def _glu_kernel(a_ref, w1_ref, w3_ref, o_ref):
    a = a_ref[...]
    g = jnp.dot(a, w1_ref[...], preferred_element_type=F32)
    u = jnp.dot(a, w3_ref[...], preferred_element_type=F32)
    o_ref[...] = (_silu(g) * u).astype(o_ref.dtype)


def _pick_tiles(m, n, k_total, n_w, out_bytes):
    best = None
    for tm in (1024, 512, 256, 128):
        if m % tm:
            continue
        for tn in (1024, 768, 512, 256, 128):
            if n % tn:
                continue
            need = 2 * (tm * k_total * 2 + k_total * tn * 2 * n_w + tm * tn * out_bytes)
            if need <= MATMUL_VMEM_BUDGET and (best is None or tm * tn > best[0] * best[1]):
                best = (tm, tn)
    assert best is not None, (m, n, k_total)
    return best


def _matmul(a_list, w_list, out_dtype, name):
    m, n = a_list[0].shape[0], w_list[0].shape[1]
    k_total = sum(a.shape[1] for a in a_list)
    tm, tn = _pick_tiles(m, n, k_total, 1, jnp.dtype(out_dtype).itemsize)
    in_specs = ([pl.BlockSpec((tm, a.shape[1]), lambda j, i: (i, 0)) for a in a_list]
                + [pl.BlockSpec((w.shape[0], tn), lambda j, i: (0, j)) for w in w_list])
    return pl.pallas_call(
        functools.partial(_mm_kernel, n_a=len(a_list)),
        grid=(n // tn, m // tm), in_specs=in_specs,
        out_specs=pl.BlockSpec((tm, tn), lambda j, i: (i, j)),
        out_shape=jax.ShapeDtypeStruct((m, n), out_dtype),
        compiler_params=_cparams("parallel", "arbitrary"), name=name)(*a_list, *w_list)


def _swiglu_up(a, w1, w3, name):
    m, k = a.shape
    n = w1.shape[1]
    tm, tn = _pick_tiles(m, n, k, 2, 2)
    return pl.pallas_call(
        _glu_kernel, grid=(n // tn, m // tm),
        in_specs=[pl.BlockSpec((tm, k), lambda j, i: (i, 0)),
                  pl.BlockSpec((k, tn), lambda j, i: (0, j)),
                  pl.BlockSpec((k, tn), lambda j, i: (0, j))],
        out_specs=pl.BlockSpec((tm, tn), lambda j, i: (i, j)),
        out_shape=jax.ShapeDtypeStruct((m, n), BF16),
        compiler_params=_cparams("parallel", "arbitrary"), name=name)(a, w1, w3)


def _moe_up_kernel(a_ref, w1_ref, w3_ref, o_ref):
    a = a_ref[...]
    g = jnp.dot(a, w1_ref[0], preferred_element_type=F32)
    u = jnp.dot(a, w3_ref[0], preferred_element_type=F32)
    o_ref[0] = (_silu(g) * u).astype(o_ref.dtype)


def _moe_down_kernel(act_ref, w2_ref, comb_ref, o_ref):
    e = pl.program_id(2)

    @pl.when(e == 0)
    def _():
        o_ref[...] = jnp.zeros_like(o_ref)

    comb = comb_ref[...]
    lane = lax.broadcasted_iota(jnp.int32, comb.shape, 1)
    gate = jnp.sum(jnp.where(lane == e, comb, 0.0), axis=-1, keepdims=True)
    o_ref[...] += gate * jnp.dot(act_ref[0], w2_ref[0], preferred_element_type=F32)


def _moe_dense(h, comb, w1, w3, w2):
    m, k = h.shape
    ne, _, f = w1.shape
    tm, tn = _pick_tiles(m, f, k, 2, 2)
    act = pl.pallas_call(
        _moe_up_kernel, grid=(ne, f // tn, m // tm),
        in_specs=[pl.BlockSpec((tm, k), lambda e, j, i: (i, 0)),
                  pl.BlockSpec((1, k, tn), lambda e, j, i: (e, 0, j)),
                  pl.BlockSpec((1, k, tn), lambda e, j, i: (e, 0, j))],
        out_specs=pl.BlockSpec((1, tm, tn), lambda e, j, i: (e, i, j)),
        out_shape=jax.ShapeDtypeStruct((ne, m, f), BF16),
        compiler_params=_cparams("parallel", "parallel", "arbitrary"), name="moe_up")(h, w1, w3)
    tm2, tn2 = _pick_tiles(m, k, f, 1, 4)
    return pl.pallas_call(
        _moe_down_kernel, grid=(m // tm2, k // tn2, ne),
        in_specs=[pl.BlockSpec((1, tm2, f), lambda i, j, e: (e, i, 0)),
                  pl.BlockSpec((1, f, tn2), lambda i, j, e: (e, 0, j)),
                  pl.BlockSpec((tm2, LANE), lambda i, j, e: (i, 0))],
        out_specs=pl.BlockSpec((tm2, tn2), lambda i, j, e: (i, j)),
        out_shape=jax.ShapeDtypeStruct((m, k), F32),
        compiler_params=_cparams("parallel", "parallel", "arbitrary"), name="moe_down")(act, w2, comb)


def _tile_specs(width, col_block, n_tiles):
    per = TT // HALO
    last = n_tiles * per - 1
    cur = pl.BlockSpec((1, TT, width), lambda b, i: (b, i, col_block))
    prev = pl.BlockSpec((1, HALO, width), lambda b, i: (b, jnp.maximum(i * per - 1, 0), col_block))
    nxt = pl.BlockSpec((1, HALO, width), lambda b, i: (b, jnp.minimum((i + 1) * per, last), col_block))
    return cur, prev, nxt


def _segment_edges(i, nct, nt):
    has_prev = jnp.logical_and(i != 0, i != nct)
    has_next = jnp.logical_and(i != nct - 1, i != nt - 1)
    return has_prev, has_next


def _fill_ext(ext_ref, cur, prev, nxt, has_prev, has_next):
    ext_ref[0:HALO, :] = jnp.where(has_prev, prev, 0.0)
    ext_ref[HALO:HALO + TT, :] = cur
    ext_ref[HALO + TT:HALO + TT + HALO, :] = jnp.where(has_next, nxt, 0.0)


def _dwconv_cols(ext_ref, w_ref, b_ref, taps, pad_l, c0, width):
    acc = jnp.broadcast_to(b_ref[:, pl.ds(c0, width)], (TT, width))
    for k in range(taps):
        off = HALO + k - pad_l
        acc = acc + ext_ref[off:off + TT, pl.ds(c0, width)] * w_ref[k:k + 1, pl.ds(c0, width)]
    return acc


def _scan_tile_index(nct, nt, reverse):
    if not reverse:
        return lambda i: i
    return lambda i: jnp.where(i < nct, nct - 1 - i, nt + nct - 1 - i)


CONV_COLS = 256


def _ssd_prep_kernel(xs_ref, xsp_ref, xsn_ref, bc_ref, bcp_ref, bcn_ref, dt_ref, wx_ref, bx_ref, wb_ref, bb_ref,
                     dtb_ref, xo_ref, bo_ref, dto_ref, extx_ref, extb_ref, *, nct, nt):
    i = pl.program_id(1)
    has_prev, has_next = _segment_edges(i, nct, nt)
    _fill_ext(extx_ref, xs_ref[0], xsp_ref[0], xsn_ref[0], has_prev, has_next)
    _fill_ext(extb_ref, bc_ref[0], bcp_ref[0], bcn_ref[0], has_prev, has_next)
    for c0 in range(0, D, CONV_COLS):
        xo_ref[0, :, c0:c0 + CONV_COLS] = _silu(_dwconv_cols(extx_ref, wx_ref, bx_ref, SHORT_CONV, SHORT_PAD_L, c0, CONV_COLS))
    for c0 in range(0, 2 * SSD_GROUPS * SSD_STATE, CONV_COLS):
        bo_ref[0, :, c0:c0 + CONV_COLS] = _silu(_dwconv_cols(extb_ref, wb_ref, bb_ref, SHORT_CONV, SHORT_PAD_L, c0, CONV_COLS))
    dto_ref[0] = _softplus(dt_ref[0] + dtb_ref[...])


def _ssd_prep(proj, conv_w, conv_b, dt_bias, nct, nt):
    nb, tt, _ = proj.shape
    nbc = 2 * SSD_GROUPS * SSD_STATE
    xs_specs = _tile_specs(D, AB_XS // D, nt)
    bc_specs = _tile_specs(nbc, AB_BC // nbc, nt)
    full = lambda shape: pl.BlockSpec(shape, lambda b, i: (0,) * len(shape))
    dtb = jnp.zeros((1, LANE), F32).at[0, :2 * SSD_HEADS].set(dt_bias.reshape(-1))
    return pl.pallas_call(
        functools.partial(_ssd_prep_kernel, nct=nct, nt=nt),
        grid=(nb, nt),
        in_specs=[*xs_specs, *bc_specs,
                  pl.BlockSpec((1, TT, LANE), lambda b, i: (b, i, AB_DT // LANE)),
                  full((SHORT_CONV, D)), full((1, D)), full((SHORT_CONV, nbc)), full((1, nbc)), full((1, LANE))],
        out_specs=[pl.BlockSpec((1, TT, D), lambda b, i: (b, i, 0)),
                   pl.BlockSpec((1, TT, nbc), lambda b, i: (b, i, 0)),
                   pl.BlockSpec((1, TT, LANE), lambda b, i: (b, i, 0))],
        out_shape=[jax.ShapeDtypeStruct((nb, tt, D), F32), jax.ShapeDtypeStruct((nb, tt, nbc), F32),
                   jax.ShapeDtypeStruct((nb, tt, LANE), F32)],
        scratch_shapes=[pltpu.VMEM((TT + 2 * HALO, D), F32), pltpu.VMEM((TT + 2 * HALO, nbc), F32)],
        compiler_params=_cparams("parallel", "arbitrary"), name="ssd_prep",
    )(proj, proj, proj, proj, proj, proj, proj, conv_w[:, :D], conv_b[None, :D], conv_w[:, D:], conv_b[None, D:], dtb)


def _ssd_scan_kernel(*refs, direction, final):
    if final:
        xs_ref, bc_ref, dt_ref, aneg_ref, tri_ref, yin_ref, z_ref, nw_ref, o_ref, h_ref, y_ref = refs
    else:
        xs_ref, bc_ref, dt_ref, aneg_ref, tri_ref, dsk_ref, o_ref, h_ref = refs
    forward = direction == 0

    @pl.when(pl.program_id(1) == 0)
    def _():
        h_ref[...] = jnp.zeros_like(h_ref)

    dt = dt_ref[0]
    d_a = dt * aneg_ref[...]
    a_cs = jnp.dot(tri_ref[...], d_a, preferred_element_type=F32, precision=HIGHEST)
    end = TT - 1 if forward else 0
    tot = a_cs[end:end + 1, :]
    w_end = dt * jnp.exp(tot - a_cs)
    e_in = jnp.exp(a_cs)
    e_tot = jnp.exp(tot)
    a_cs_t, w_end_t, dt_t = a_cs.T, w_end.T, dt.T
    row = lax.broadcasted_iota(jnp.int32, (TT, TT), 0)
    col = lax.broadcasted_iota(jnp.int32, (TT, TT), 1)
    causal = (row >= col) if forward else (row <= col)
    low = col < SSD_HEAD_DIM
    neg = jnp.float32(-jnp.inf)
    ns = SSD_GROUPS * SSD_STATE
    pair_w = 2 * SSD_HEAD_DIM

    for g in range(SSD_GROUPS):
        b_g = bc_ref[0, :, g * SSD_STATE:(g + 1) * SSD_STATE]
        c_g = bc_ref[0, :, ns + g * SSD_STATE:ns + (g + 1) * SSD_STATE].astype(BF16)
        scores = lax.dot_general(c_g, b_g.astype(BF16), (((1,), (1,)), ((), ())), preferred_element_type=F32)
        b_t = b_g.T
        gw = SSD_GROUP_HEADS * SSD_HEAD_DIM
        y_off = jnp.dot(c_g, h_ref[:, g * gw:(g + 1) * gw].astype(BF16), preferred_element_type=F32)
        for pr in range(SSD_GROUP_HEADS // 2):
            head = g * SSD_GROUP_HEADS + 2 * pr
            ca = direction * SSD_HEADS + head
            cb = ca + 1
            c0 = head * SSD_HEAD_DIM
            xs_pair = xs_ref[0, :, c0:c0 + pair_w]
            rhs = jnp.concatenate([jnp.where(low, xs_pair, 0.0), jnp.where(low, 0.0, xs_pair)], axis=0).astype(BF16)

            def head_lhs(c):
                seg = a_cs[:, c:c + 1] - a_cs_t[c:c + 1, :]
                m = scores * jnp.exp(jnp.where(causal, seg, neg)) * dt_t[c:c + 1, :]
                return m, b_t * w_end_t[c:c + 1, :]

            m_a, bw_a = head_lhs(ca)
            m_b, bw_b = head_lhs(cb)
            lhs = jnp.concatenate([jnp.concatenate([m_a, m_b], axis=1),
                                   jnp.concatenate([bw_a, bw_b], axis=1)], axis=0).astype(BF16)
            res = jnp.dot(lhs, rhs, preferred_element_type=F32)
            e_pair = jnp.where(low, e_in[:, ca:ca + 1], e_in[:, cb:cb + 1])
            y_pair = res[:TT] + y_off[:, 2 * pr * SSD_HEAD_DIM:2 * pr * SSD_HEAD_DIM + pair_w] * e_pair
            et_pair = jnp.where(low[0:1], e_tot[:, ca:ca + 1], e_tot[:, cb:cb + 1])
            h_ref[:, c0:c0 + pair_w] = h_ref[:, c0:c0 + pair_w] * et_pair + res[TT:]
            if final:
                y_ref[:, c0:c0 + pair_w] = yin_ref[0, :, c0:c0 + pair_w] + y_pair
            else:
                o_ref[0, :, c0:c0 + pair_w] = xs_pair * dsk_ref[:, c0:c0 + pair_w] + y_pair

    if final:
        u = y_ref[...] * _silu(z_ref[0])
        o_ref[0] = (u * lax.rsqrt(jnp.mean(u * u, axis=-1, keepdims=True) + LN_EPS) * nw_ref[...]).astype(o_ref.dtype)


def _ssd_scan(xs_c, bc_c, dt_sp, a_log, direction, nct, nt, *, d_skip=None, y_in=None, proj=None, norm_w=None):
    nb, tt, _ = xs_c.shape
    nbc = bc_c.shape[-1]
    final = y_in is not None
    tile_of = _scan_tile_index(nct, nt, direction == 1)
    tok = lambda w, cb=0: pl.BlockSpec((1, TT, w), lambda b, i: (b, tile_of(i), cb))
    full = lambda shape: pl.BlockSpec(shape, lambda b, i: (0,) * len(shape))
    aneg = jnp.zeros((1, LANE), F32).at[0, :2 * SSD_HEADS].set(-jnp.exp(a_log.astype(F32)).reshape(-1))
    ones = np.ones((TT, TT), np.float32)
    tri = jnp.asarray(np.tril(ones) if direction == 0 else np.triu(ones))
    args = [xs_c, bc_c, dt_sp, aneg, tri]
    in_specs = [tok(D), tok(nbc), tok(LANE), full((1, LANE)), full((TT, TT))]
    scratch = [pltpu.VMEM((SSD_STATE, D), F32)]
    if final:
        args += [y_in, proj, norm_w[None, :]]
        in_specs += [tok(D), tok(D, AB_Z // D), full((1, D))]
        scratch.append(pltpu.VMEM((TT, D), F32))
        out_dtype = BF16
    else:
        args.append(jnp.repeat(d_skip.astype(F32), SSD_HEAD_DIM)[None, :])
        in_specs.append(full((1, D)))
        out_dtype = F32
    return pl.pallas_call(
        functools.partial(_ssd_scan_kernel, direction=direction, final=final),
        grid=(nb, nt), in_specs=in_specs, out_specs=tok(D),
        out_shape=jax.ShapeDtypeStruct((nb, tt, D), out_dtype), scratch_shapes=scratch,
        compiler_params=_cparams("parallel", "arbitrary"), name=f"ssd_scan{direction}")(*args)


def _rope(x, cos, sin):
    lane = lax.broadcasted_iota(jnp.int32, x.shape, 1)
    quarter = HEAD_DIM // 4
    partner = jnp.where((lane & quarter) == 0, pltpu.roll(x, HEAD_DIM - quarter, 1), pltpu.roll(x, quarter, 1))
    return x * cos + partner * sin


def _attn_kernel(sink_ref, q_ref, kp_ref, kc_ref, kn_ref, vp_ref, vc_ref, vn_ref, kx_ref, vx_ref,
                 cp_ref, cc_ref, cn_ref, sp_ref, sc_ref, sn_ref, o_ref, *, nct, nt):
    g = pl.program_id(1)
    i = pl.program_id(2)
    nt_dims = (((1,), (1,)), ((), ()))
    rows = ATTN_GROUP * TT
    sink_col = jnp.concatenate(
        [jnp.full((TT, 1), sink_ref[g * ATTN_GROUP + j], F32) for j in range(ATTN_GROUP)], axis=0)
    k_ctx = kx_ref[0].astype(BF16)
    v_ctx = vx_ref[0].astype(BF16)

    def finish(o, den):
        o = o / den
        for j in range(ATTN_GROUP):
            o_ref[0, :, j * HEAD_DIM:(j + 1) * HEAD_DIM] = o[j * TT:(j + 1) * TT].astype(o_ref.dtype)

    @pl.when(i >= nct)
    def _():
        cos_c, sin_c = cc_ref[...], sc_ref[...]
        q = jnp.concatenate([_rope(q_ref[0, :, j * HEAD_DIM:(j + 1) * HEAD_DIM], cos_c, sin_c)
                             for j in range(ATTN_GROUP)], axis=0).astype(BF16)
        k_loc = jnp.concatenate([_rope(kp_ref[0], cp_ref[...], sp_ref[...]), _rope(kc_ref[0], cos_c, sin_c),
                                 _rope(kn_ref[0], cn_ref[...], sn_ref[...])], axis=0).astype(BF16)
        v_loc = jnp.concatenate([vp_ref[0], vc_ref[0], vn_ref[0]], axis=0).astype(BF16)
        s_loc = lax.dot_general(q, k_loc, nt_dims, preferred_element_type=F32) * ATTN_SCALE
        a = lax.broadcasted_iota(jnp.int32, (rows, 3 * TT), 0) & (TT - 1)
        j = lax.broadcasted_iota(jnp.int32, (rows, 3 * TT), 1)
        lo = jnp.where(i > nct, 0, TT)
        hi = jnp.where(i < nt - 1, 3 * TT, 2 * TT)
        valid = (j >= a) & (j <= a + 2 * WINDOW) & (j >= lo) & (j < hi)
        s_loc = jnp.where(valid, s_loc, jnp.float32(-jnp.inf))
        s_ctx = lax.dot_general(q, k_ctx, nt_dims, preferred_element_type=F32) * ATTN_SCALE
        m = jnp.maximum(jnp.maximum(jnp.max(s_loc, axis=-1, keepdims=True), jnp.max(s_ctx, axis=-1, keepdims=True)),
                        sink_col)
        p_loc = jnp.exp(s_loc - m)
        p_ctx = jnp.exp(s_ctx - m)
        den = jnp.sum(p_loc, axis=-1, keepdims=True) + jnp.sum(p_ctx, axis=-1, keepdims=True) + jnp.exp(sink_col - m)
        o = (jnp.dot(p_loc.astype(BF16), v_loc, preferred_element_type=F32)
             + jnp.dot(p_ctx.astype(BF16), v_ctx, preferred_element_type=F32))
        finish(o, den)

    @pl.when(i < nct)
    def _():
        q = jnp.concatenate([q_ref[0, :, j * HEAD_DIM:(j + 1) * HEAD_DIM] for j in range(ATTN_GROUP)],
                            axis=0).astype(BF16)
        s_ctx = lax.dot_general(q, k_ctx, nt_dims, preferred_element_type=F32) * ATTN_SCALE
        m = jnp.maximum(jnp.max(s_ctx, axis=-1, keepdims=True), sink_col)
        p_ctx = jnp.exp(s_ctx - m)
        den = jnp.sum(p_ctx, axis=-1, keepdims=True) + jnp.exp(sink_col - m)
        finish(jnp.dot(p_ctx.astype(BF16), v_ctx, preferred_element_type=F32), den)


def _rope_tables(n_ctx, s):
    quarter = HEAD_DIM // 4
    inv_freq = ROPE_BASE ** (-jnp.arange(quarter, dtype=F32) / quarter)
    pos = jnp.arange(s)
    ang_r = (pos // GRID_W).astype(F32)[:, None] * inv_freq[None, :]
    ang_c = (pos % GRID_W).astype(F32)[:, None] * inv_freq[None, :]
    cos = jnp.concatenate([jnp.cos(ang_r), jnp.cos(ang_r), jnp.cos(ang_c), jnp.cos(ang_c)], axis=-1)
    sin = jnp.concatenate([-jnp.sin(ang_r), jnp.sin(ang_r), -jnp.sin(ang_c), jnp.sin(ang_c)], axis=-1)
    pad = jnp.zeros((n_ctx, HEAD_DIM), F32)
    return jnp.concatenate([pad, cos], axis=0), jnp.concatenate([pad, sin], axis=0)


def _attention(proj, sink, cos, sin, n_ctx, nct, nt):
    nb, tt, _ = proj.shape
    qw = ATTN_GROUP * HEAD_DIM
    prev_i = lambda i: jnp.maximum(i - 1, 0)
    next_i = lambda i: jnp.minimum(i + 1, nt - 1)

    def kv(base, which):
        return pl.BlockSpec((1, TT, HEAD_DIM), lambda b, g, i: (b, which(i), base // HEAD_DIM + g))

    def tab(which):
        return pl.BlockSpec((TT, HEAD_DIM), lambda b, g, i: (which(i), 0))

    same = lambda i: i
    in_specs = [pl.BlockSpec(memory_space=pltpu.SMEM),
                pl.BlockSpec((1, TT, qw), lambda b, g, i: (b, i, AB_Q // qw + g)),
                kv(AB_K, prev_i), kv(AB_K, same), kv(AB_K, next_i),
                kv(AB_V, prev_i), kv(AB_V, same), kv(AB_V, next_i),
                pl.BlockSpec((1, n_ctx, HEAD_DIM), lambda b, g, i: (b, 0, AB_K // HEAD_DIM + g)),
                pl.BlockSpec((1, n_ctx, HEAD_DIM), lambda b, g, i: (b, 0, AB_V // HEAD_DIM + g)),
                tab(prev_i), tab(same), tab(next_i), tab(prev_i), tab(same), tab(next_i)]
    return pl.pallas_call(
        functools.partial(_attn_kernel, nct=nct, nt=nt),
        grid=(nb, ATTN_KV, nt), in_specs=in_specs,
        out_specs=pl.BlockSpec((1, TT, qw), lambda b, g, i: (b, i, g)),
        out_shape=jax.ShapeDtypeStruct((nb, tt, D), BF16),
        compiler_params=_cparams("parallel", "parallel", "arbitrary"), name="attention",
    )(sink.astype(F32), proj, proj, proj, proj, proj, proj, proj, proj, proj, cos, cos, cos, sin, sin, sin)


def _conformer_kernel(a_ref, ap_ref, an_ref, b_ref, bp_ref, bn_ref, w_ref, cb_ref, ln_ref, o_ref, ext_ref, acc_ref,
                      *, nct, nt):
    i = pl.program_id(1)
    has_prev, has_next = _segment_edges(i, nct, nt)
    _fill_ext(ext_ref, a_ref[0] * _sigmoid(b_ref[0]), ap_ref[0] * _sigmoid(bp_ref[0]), an_ref[0] * _sigmoid(bn_ref[0]),
              has_prev, has_next)

    def cols(c, carry):
        c0 = pl.multiple_of(c * CONV_COLS, CONV_COLS)
        acc_ref[:, pl.ds(c0, CONV_COLS)] = _dwconv_cols(ext_ref, w_ref, cb_ref, CONF_KERNEL, CONF_PAD_L, c0, CONV_COLS)
        return carry

    lax.fori_loop(0, D // CONV_COLS, cols, 0)
    u = acc_ref[...]
    mu = jnp.mean(u, axis=-1, keepdims=True)
    uc = u - mu
    var = jnp.mean(uc * uc, axis=-1, keepdims=True)
    o_ref[0] = _silu(uc * lax.rsqrt(var + LN_EPS) * ln_ref[0:1, :] + ln_ref[1:2, :]).astype(o_ref.dtype)


def _conformer(proj, dw_w, dw_b, ln_g, ln_b, nct, nt):
    nb, tt, _ = proj.shape
    full = lambda shape: pl.BlockSpec(shape, lambda b, i: (0,) * len(shape))
    return pl.pallas_call(
        functools.partial(_conformer_kernel, nct=nct, nt=nt),
        grid=(nb, nt),
        in_specs=[*_tile_specs(D, CD_GLU_A // D, nt), *_tile_specs(D, CD_GLU_B // D, nt),
                  full((CONF_KERNEL, D)), full((1, D)), full((2, D))],
        out_specs=pl.BlockSpec((1, TT, D), lambda b, i: (b, i, 0)),
        out_shape=jax.ShapeDtypeStruct((nb, tt, D), BF16),
        scratch_shapes=[pltpu.VMEM((TT + 2 * HALO, D), F32), pltpu.VMEM((TT, D), F32)],
        compiler_params=_cparams("parallel", "arbitrary"), name="conformer",
    )(proj, proj, proj, proj, proj, proj, dw_w, dw_b[None, :], jnp.stack([ln_g, ln_b]))


def _rg_prep_kernel(x_ref, xp_ref, xn_ref, w_ref, cb_ref, gw_ref, gb_ref, lam_ref, af_ref, bf_ref, ab_ref, bb_ref,
                    ext_ref, *, nct, nt):
    i = pl.program_id(1)
    has_prev, has_next = _segment_edges(i, nct, nt)
    _fill_ext(ext_ref, x_ref[0], xp_ref[0], xn_ref[0], has_prev, has_next)
    outs = ((af_ref, bf_ref), (ab_ref, bb_ref))
    for blk in range(RG_BLOCKS):
        c0 = blk * RG_BLOCK
        xr = _dwconv_cols(ext_ref, w_ref, cb_ref, SHORT_CONV, SHORT_PAD_L, c0, RG_BLOCK)
        pre = jnp.dot(xr.astype(BF16), gw_ref[blk], preferred_element_type=F32) + gb_ref[blk:blk + 1, :]
        for d in range(2):
            r = _sigmoid(pre[:, (2 * d) * RG_BLOCK:(2 * d + 1) * RG_BLOCK])
            gate_i = _sigmoid(pre[:, (2 * d + 1) * RG_BLOCK:(2 * d + 2) * RG_BLOCK])
            log_a = -RG_C * r * _softplus(-lam_ref[d:d + 1, c0:c0 + RG_BLOCK])
            outs[d][0][0, :, c0:c0 + RG_BLOCK] = jnp.exp(log_a)
            outs[d][1][0, :, c0:c0 + RG_BLOCK] = jnp.sqrt(-_expm1(2.0 * log_a)) * gate_i * xr


def _rg_prep(proj, conv_w, conv_b, gate_w, gate_b, lam, nct, nt):
    nb, tt, _ = proj.shape
    full = lambda shape: pl.BlockSpec(shape, lambda b, i: (0,) * len(shape))
    gw = jnp.transpose(gate_w, (2, 3, 0, 1, 4)).reshape(RG_BLOCKS, RG_BLOCK, 4 * RG_BLOCK).astype(BF16)
    gb = jnp.transpose(gate_b.reshape(2, 2, RG_BLOCKS, RG_BLOCK), (2, 0, 1, 3)).reshape(RG_BLOCKS, 4 * RG_BLOCK)
    tok = pl.BlockSpec((1, TT, D), lambda b, i: (b, i, 0))
    shp = jax.ShapeDtypeStruct((nb, tt, D), F32)
    return pl.pallas_call(
        functools.partial(_rg_prep_kernel, nct=nct, nt=nt),
        grid=(nb, nt),
        in_specs=[*_tile_specs(D, CD_RX // D, nt), full((SHORT_CONV, D)), full((1, D)),
                  full((RG_BLOCKS, RG_BLOCK, 4 * RG_BLOCK)), full((RG_BLOCKS, 4 * RG_BLOCK)), full((2, D))],
        out_specs=[tok, tok, tok, tok], out_shape=[shp, shp, shp, shp],
        scratch_shapes=[pltpu.VMEM((TT + 2 * HALO, D), F32)],
        compiler_params=_cparams("parallel", "arbitrary"), name="rg_prep",
    )(proj, proj, proj, conv_w, conv_b[None, :], gw, gb.astype(F32), lam.astype(F32))


def _rg_scan_kernel(*refs, reverse, final):
    if final:
        a_ref, b_ref, hf_ref, gate_ref, o_ref, h_ref, hs_ref = refs
    else:
        a_ref, b_ref, o_ref, h_ref, hs_ref = refs

    @pl.when(pl.program_id(1) == 0)
    def _():
        h_ref[...] = jnp.zeros_like(h_ref)

    def step(s, h):
        t = (TT - 1 - s) if reverse else s
        h = a_ref[0, pl.ds(t, 1), :] * h + b_ref[0, pl.ds(t, 1), :]
        hs_ref[pl.ds(t, 1), :] = h
        return h

    h_ref[...] = lax.fori_loop(0, TT, step, h_ref[...])
    if final:
        o_ref[0] = ((hf_ref[0] + hs_ref[...]) * _gelu_tanh(gate_ref[0])).astype(o_ref.dtype)
    else:
        o_ref[0] = hs_ref[...]


def _rg_scan(a, b, reverse, nct, nt, *, h_fwd=None, proj=None):
    nb, tt, _ = a.shape
    final = h_fwd is not None
    tile_of = _scan_tile_index(nct, nt, reverse)
    tok = lambda cb=0: pl.BlockSpec((1, TT, D), lambda bb, i: (bb, tile_of(i), cb))
    args, in_specs = [a, b], [tok(), tok()]
    if final:
        args += [h_fwd, proj]
        in_specs += [tok(), tok(CD_RGATE // D)]
    return pl.pallas_call(
        functools.partial(_rg_scan_kernel, reverse=reverse, final=final),
        grid=(nb, nt), in_specs=in_specs, out_specs=tok(),
        out_shape=jax.ShapeDtypeStruct((nb, tt, D), BF16 if final else F32),
        scratch_shapes=[pltpu.VMEM((1, D), F32), pltpu.VMEM((TT, D), F32)],
        compiler_params=_cparams("parallel", "arbitrary"), name=f"rg_scan{int(reverse)}")(*args)


def _even_mixer(h, w_in, w_out, conv_w, conv_b, dt_bias, a_log, d_skip, norm_w, sink, cos, sin, n_ctx, nct, nt):
    nb, tt, _ = h.shape
    w = w_in
    w_perm = jnp.concatenate([w[:, 2048:4096], w[:, 0:2048], w[:, 5184:7232], w[:, 4096:5120], w[:, 7232:7744],
                              w[:, 7744:8256], w[:, 5120:5184], jnp.zeros((D, AB_N - 8256), w.dtype)], axis=1)
    proj = _matmul([h.reshape(nb * tt, D)], [w_perm.astype(BF16)], F32, "ab_in").reshape(nb, tt, AB_N)
    xs_c, bc_c, dt_sp = _ssd_prep(proj, conv_w, conv_b, dt_bias, nct, nt)
    y_fwd = _ssd_scan(xs_c, bc_c, dt_sp, a_log, 0, nct, nt, d_skip=d_skip)
    ssd_o = _ssd_scan(xs_c, bc_c, dt_sp, a_log, 1, nct, nt, y_in=y_fwd, proj=proj, norm_w=norm_w)
    att_o = _attention(proj, sink, cos, sin, n_ctx, nct, nt)
    wo = w_out.astype(BF16)
    return _matmul([ssd_o.reshape(nb * tt, D), att_o.reshape(nb * tt, D)], [wo[:D], wo[D:]], F32, "ab_out")


def _odd_mixer(h, w_in, w_out, dw_w, dw_b, cln_g, cln_b, rconv_w, rconv_b, gate_w, gate_b, lam, nct, nt):
    nb, tt, _ = h.shape
    proj = _matmul([h.reshape(nb * tt, D)], [w_in.astype(BF16)], F32, "cd_in").reshape(nb, tt, CD_N)
    conf_o = _conformer(proj, dw_w, dw_b, cln_g, cln_b, nct, nt)
    a_f, b_f, a_b, b_b = _rg_prep(proj, rconv_w, rconv_b, gate_w, gate_b, lam, nct, nt)
    h_fwd = _rg_scan(a_f, b_f, False, nct, nt)
    rg_o = _rg_scan(a_b, b_b, True, nct, nt, h_fwd=h_fwd, proj=proj)
    wo = w_out.astype(BF16)
    return _matmul([conf_o.reshape(nb * tt, D), rg_o.reshape(nb * tt, D)], [wo[:D], wo[D:]], F32, "cd_out")


def kernel(x, c, ctx, c_ctx, mod_w, mod_b, ln_g, ln_b, ab_w_in, ab_w_out, ssd_conv_w, ssd_conv_b, ssd_dt_bias, ssd_a_log, ssd_d, ssd_norm_w, attn_sink, ffn_w1, ffn_w3, ffn_w2, cd_w_in, cd_w_out, conf_dw_w, conf_dw_b, conf_ln_g, conf_ln_b, rg_conv_w, rg_conv_b, rg_gate_w, rg_gate_b, rg_lambda, moe_router, moe_w1, moe_w3, moe_w2):
    nb, s, _ = x.shape
    n_ctx = ctx.shape[1]
    depth = mod_w.shape[0]
    assert s % TT == 0 and n_ctx % TT == 0 and n_ctx >= TT and x.shape[2] == D
    tt = n_ctx + s
    nct, nt = n_ctx // TT, tt // TT
    alpha = (2 * depth) ** 0.25

    modt = _modulation_tables(c, c_ctx, mod_w, mod_b)
    cos, sin = _rope_tables(n_ctx, s)
    xs = jnp.concatenate([ctx, x], axis=1)
    (h,) = _post(xs, n_ctx, nxt=(modt[0], 0, 1))
    for l in range(depth):
        j = l // 2
        if l % 2 == 0:
            mix = _even_mixer(h, ab_w_in[j], ab_w_out[j], ssd_conv_w[j], ssd_conv_b[j], ssd_dt_bias[j], ssd_a_log[j],
                              ssd_d[j], ssd_norm_w[j], attn_sink[j], cos, sin, n_ctx, nct, nt)
        else:
            mix = _odd_mixer(h, cd_w_in[j], cd_w_out[j], conf_dw_w[j], conf_dw_b[j], conf_ln_g[j], conf_ln_b[j],
                             rg_conv_w[j], rg_conv_b[j], rg_gate_w[j], rg_gate_b[j], rg_lambda[j], nct, nt)
        ln0 = jnp.stack([ln_g[l, 0], ln_b[l, 0]])
        ln1 = jnp.stack([ln_g[l, 1], ln_b[l, 1]])
        res = (mix.reshape(nb, tt, D), modt[l], 2, ln0, alpha)
        hf = h.reshape(nb * tt, D)
        if l % 2 == 0:
            xs, h = _post(xs, n_ctx, res=res, nxt=(modt[l], 3, 4))
            hf = h.reshape(nb * tt, D)
            act = _swiglu_up(hf, ffn_w1[j].astype(BF16), ffn_w3[j].astype(BF16), "ffn_up")
            f = _matmul([act], [ffn_w2[j].astype(BF16)], F32, "ffn_down")
        else:
            wr = jnp.zeros((D, LANE), F32).at[:, :N_EXPERTS].set(moe_router[j])
            xs, h, comb = _post(xs, n_ctx, res=res, nxt=(modt[l], 3, 4), router=wr)
            hf = h.reshape(nb * tt, D)
            f = _moe_dense(hf, comb.reshape(nb * tt, LANE), moe_w1[j].astype(BF16), moe_w3[j].astype(BF16),
                           moe_w2[j].astype(BF16))
        res = (f.reshape(nb, tt, D), modt[l], 5, ln1, alpha)
        if l + 1 < depth:
            xs, h = _post(xs, n_ctx, res=res, nxt=(modt[l + 1], 0, 1))
        else:
            (xs,) = _post(xs, n_ctx, res=res)
    return xs[:, n_ctx:]
```

```python
import functools
import math

import jax
import jax.numpy as jnp
import numpy as np
from jax import lax
from jax.experimental import pallas as pl
from jax.experimental.pallas import tpu as pltpu

F32 = jnp.float32
BF16 = jnp.bfloat16
HIGHEST = lax.Precision.HIGHEST

LANE = 128
SUBLANE = 8
VMEM_BYTES = 64 * 1024 * 1024
VMEM_LIMIT = VMEM_BYTES * 7 // 8
MATMUL_VMEM_BUDGET = VMEM_BYTES * 5 // 8

D = 2048
GRID_W = 64
SSD_HEAD_DIM = 64
SSD_HEADS = D // SSD_HEAD_DIM
SSD_GROUPS = 4
SSD_STATE = 128
SSD_GROUP_HEADS = SSD_HEADS // SSD_GROUPS
SHORT_CONV = 4
SHORT_PAD_L = 2
HEAD_DIM = 128
ATTN_HEADS = D // HEAD_DIM
ATTN_KV = ATTN_HEADS // 4
ATTN_GROUP = ATTN_HEADS // ATTN_KV
WINDOW = 128
ATTN_SCALE = HEAD_DIM ** -0.5
ROPE_BASE = 10000.0
CONF_KERNEL = 31
CONF_PAD_L = 15
RG_BLOCK = 128
RG_BLOCKS = D // RG_BLOCK
RG_C = 8.0
N_EXPERTS = 8
ROUTE_I1, ROUTE_I2 = N_EXPERTS, N_EXPERTS + 1
LN_EPS = 1e-5

TT = 128
HALO = 16

AB_XS, AB_Z, AB_Q, AB_BC, AB_K, AB_V, AB_DT = 0, 2048, 4096, 6144, 7168, 7680, 8192
AB_N = 8448
CD_GLU_A, CD_GLU_B, CD_RX, CD_RGATE = 0, 2048, 4096, 6144
CD_N = 8192


def _cparams(*sem):
    return pltpu.CompilerParams(dimension_semantics=sem, vmem_limit_bytes=VMEM_LIMIT)


def _sigmoid(x):
    return 1.0 / (1.0 + jnp.exp(-x))


def _silu(x):
    return x * _sigmoid(x)


def _softplus(x):
    return jnp.maximum(x, 0.0) + jnp.log1p(jnp.exp(-jnp.abs(x)))


def _expm1(x):
    u = jnp.exp(x)
    near = jnp.where(u == 1.0, x, (u - 1.0) * x / jnp.log(u))
    return jnp.where(jnp.abs(x) < 0.5, near, u - 1.0)


def _gelu_tanh(x):
    return 0.5 * x * (1.0 + jnp.tanh(math.sqrt(2.0 / math.pi) * (x + 0.044715 * (x * x * x))))


def _mod_kernel(c_ref, w_ref, b_ref, o_ref):
    s = _silu(c_ref[...])
    o_ref[0] = jnp.dot(s, w_ref[0], preferred_element_type=F32, precision=HIGHEST) + b_ref[0]


def _modulation_tables(c, c_ctx, mod_w, mod_b):
    depth, _, n6 = mod_w.shape
    nb = c.shape[0]
    rows = -(-(nb + 1) // SUBLANE) * SUBLANE
    cc = jnp.zeros((rows, D), F32).at[:nb].set(c).at[nb].set(c_ctx)
    tn = 1024
    out = pl.pallas_call(
        _mod_kernel,
        grid=(depth, n6 // tn),
        in_specs=[pl.BlockSpec((rows, D), lambda l, j: (0, 0)),
                  pl.BlockSpec((1, D, tn), lambda l, j: (l, 0, j)),
                  pl.BlockSpec((1, 1, tn), lambda l, j: (l, 0, j))],
        out_specs=pl.BlockSpec((1, rows, tn), lambda l, j: (l, 0, j)),
        out_shape=jax.ShapeDtypeStruct((depth, rows, n6), F32),
        compiler_params=_cparams("arbitrary", "arbitrary"),
        name="modulation",
    )(cc, mod_w, mod_b.reshape(depth, 1, n6))
    lat = out[:, :nb].reshape(depth, nb, 1, 6, D)
    ctx = jnp.broadcast_to(out[:, nb].reshape(depth, 1, 1, 6, D), (depth, nb, 1, 6, D))
    return jnp.concatenate([ctx, lat], axis=2)


def _post_kernel(*refs, alpha, gate_row, mod_rows, has_res, has_h, has_router):
    it = iter(refs)
    x_ref = next(it)
    if has_res:
        m_ref, modg_ref, ln_ref = next(it), next(it), next(it)
    if has_h:
        modn_ref = next(it)
    if has_router:
        wr_ref = next(it)
    if has_res:
        xo_ref = next(it)
    if has_h:
        h_ref = next(it)
    if has_router:
        comb_ref = next(it)

    x = x_ref[0]
    if has_res:
        g = modg_ref[0, 0, gate_row:gate_row + 1, :]
        xf = alpha * x + g * m_ref[0]
        mu = jnp.mean(xf, axis=-1, keepdims=True)
        xc = xf - mu
        var = jnp.mean(xc * xc, axis=-1, keepdims=True)
        x = xc * lax.rsqrt(var + LN_EPS) * ln_ref[0:1, :] + ln_ref[1:2, :]
        xo_ref[0] = x
    if has_h:
        sh = modn_ref[0, 0, mod_rows[0]:mod_rows[0] + 1, :]
        sc = modn_ref[0, 0, mod_rows[1]:mod_rows[1] + 1, :]
        h = x * (1.0 + sc) + sh
        h_ref[0] = h.astype(BF16)
        if has_router:
            logits = jnp.dot(h, wr_ref[...], preferred_element_type=F32, precision=HIGHEST)
            lane = lax.broadcasted_iota(jnp.int32, logits.shape, 1).astype(F32)
            neg = jnp.float32(-jnp.inf)
            lg = jnp.where(lane < N_EXPERTS, logits, neg)
            m1 = jnp.max(lg, axis=-1, keepdims=True)
            i1 = jnp.min(jnp.where(lg == m1, lane, float(LANE)), axis=-1, keepdims=True)
            lg2 = jnp.where(lane == i1, neg, lg)
            m2 = jnp.max(lg2, axis=-1, keepdims=True)
            i2 = jnp.min(jnp.where(lg2 == m2, lane, float(LANE)), axis=-1, keepdims=True)
            e2 = jnp.exp(m2 - m1)
            den = 1.0 + e2
            gates = jnp.where(lane == i1, 1.0 / den, 0.0) + jnp.where(lane == i2, e2 / den, 0.0)
            comb_ref[0] = jnp.where(lane == ROUTE_I1, i1, jnp.where(lane == ROUTE_I2, i2, gates))


def _post(x, n_ctx, *, res=None, nxt=None, router=None):
    nb, tt, _ = x.shape
    tm = 256 if n_ctx % 256 == 0 else TT
    nct = n_ctx // tm
    tok = pl.BlockSpec((1, tm, D), lambda b, i: (b, i, 0))
    modspec = pl.BlockSpec((1, 1, 6, D), lambda b, i: (b, jnp.where(i >= nct, 1, 0), 0, 0))
    args, in_specs, out_shape, out_specs = [x], [tok], [], []
    if res is not None:
        m, modt, gate_row, ln_gb, alpha = res
        args += [m, modt, ln_gb]
        in_specs += [tok, modspec, pl.BlockSpec((2, D), lambda b, i: (0, 0))]
        out_shape.append(jax.ShapeDtypeStruct((nb, tt, D), F32))
        out_specs.append(tok)
    else:
        gate_row, alpha = 0, 1.0
    if nxt is not None:
        args.append(nxt[0])
        in_specs.append(modspec)
        out_shape.append(jax.ShapeDtypeStruct((nb, tt, D), BF16))
        out_specs.append(tok)
    if router is not None:
        args.append(router)
        in_specs.append(pl.BlockSpec((D, LANE), lambda b, i: (0, 0)))
        out_shape.append(jax.ShapeDtypeStruct((nb, tt, LANE), F32))
        out_specs.append(pl.BlockSpec((1, tm, LANE), lambda b, i: (b, i, 0)))
    kern = functools.partial(_post_kernel, alpha=alpha, gate_row=gate_row,
                             mod_rows=(nxt[1], nxt[2]) if nxt is not None else (0, 0),
                             has_res=res is not None, has_h=nxt is not None, has_router=router is not None)
    return pl.pallas_call(kern, grid=(nb, tt // tm), in_specs=in_specs, out_specs=out_specs, out_shape=out_shape,
                          compiler_params=_cparams("parallel", "arbitrary"), name="post")(*args)


def _mm_kernel(*refs, n_a):
    o_ref = refs[-1]
    acc = jnp.dot(refs[0][...], refs[n_a][...], preferred_element_type=F32)
    for t in range(1, n_a):
        acc = acc + jnp.dot(refs[t][...], refs[n_a + t][...], preferred_element_type=F32)
    o_ref[...] = acc.astype(o_ref.dtype)


def _glu_kernel(a_ref, w1_ref, w3_ref, o_ref):
    a = a_ref[...]
    g = jnp.dot(a, w1_ref[...], preferred_element_type=F32)
    u = jnp.dot(a, w3_ref[...], preferred_element_type=F32)
    o_ref[...] = (_silu(g) * u).astype(o_ref.dtype)


def _pick_tiles(m, n, k_total, n_w, out_bytes):
    best = None
    for tm in (1024, 512, 256, 128):
        if m % tm:
            continue
        for tn in (1024, 768, 512, 256, 128):
            if n % tn:
                continue
            need = 2 * (tm * k_total * 2 + k_total * tn * 2 * n_w + tm * tn * out_bytes)
            if need <= MATMUL_VMEM_BUDGET and (best is None or tm * tn > best[0] * best[1]):
                best = (tm, tn)
    assert best is not None, (m, n, k_total)
    return best


def _matmul(a_list, w_list, out_dtype, name):
    m, n = a_list[0].shape[0], w_list[0].shape[1]
    k_total = sum(a.shape[1] for a in a_list)
    tm, tn = _pick_tiles(m, n, k_total, 1, jnp.dtype(out_dtype).itemsize)
    in_specs = ([pl.BlockSpec((tm, a.shape[1]), lambda j, i: (i, 0)) for a in a_list]
                + [pl.BlockSpec((w.shape[0], tn), lambda j, i: (0, j)) for w in w_list])
    return pl.pallas_call(
        functools.partial(_mm_kernel, n_a=len(a_list)),
        grid=(n // tn, m // tm), in_specs=in_specs,
        out_specs=pl.BlockSpec((tm, tn), lambda j, i: (i, j)),
        out_shape=jax.ShapeDtypeStruct((m, n), out_dtype),
        compiler_params=_cparams("parallel", "arbitrary"), name=name)(*a_list, *w_list)


def _swiglu_up(a, w1, w3, name):
    m, k = a.shape
    n = w1.shape[1]
    tm, tn = _pick_tiles(m, n, k, 2, 2)
    return pl.pallas_call(
        _glu_kernel, grid=(n // tn, m // tm),
        in_specs=[pl.BlockSpec((tm, k), lambda j, i: (i, 0)),
                  pl.BlockSpec((k, tn), lambda j, i: (0, j)),
                  pl.BlockSpec((k, tn), lambda j, i: (0, j))],
        out_specs=pl.BlockSpec((tm, tn), lambda j, i: (i, j)),
        out_shape=jax.ShapeDtypeStruct((m, n), BF16),
        compiler_params=_cparams("parallel", "arbitrary"), name=name)(a, w1, w3)


MOE_SUB = 512
MOE_ROWS = 512
MOE_ALIGN = 16
MOE_BLK = 64
MOE_CAT = -(-(2 * MOE_SUB + N_EXPERTS * (MOE_ALIGN - 1 + MOE_BLK - 1)) // LANE) * LANE


def _moe_rank_kernel(route_ref, tri_ref, rank_ref, cnt_ref):
    route = route_ref[...]
    lane = lax.broadcasted_iota(jnp.int32, route.shape, 1).astype(F32)
    onehot = jnp.where(lane == route[:, ROUTE_I1:ROUTE_I1 + 1], 1.0,
                       jnp.where(lane == route[:, ROUTE_I2:ROUTE_I2 + 1], 1.0, 0.0))
    rank_ref[...] = jnp.dot(tri_ref[...], onehot.astype(BF16), preferred_element_type=F32)
    cnt_ref[0] = jnp.broadcast_to(jnp.sum(onehot, axis=0, keepdims=True), (SUBLANE, LANE))


def _moe_plan(cnt, r_tiles):
    c16 = -(-cnt // MOE_ALIGN) * MOE_ALIGN
    nblk = -(-c16 // MOE_BLK)
    locw = nblk * MOE_BLK
    loc = jnp.cumsum(locw, axis=1) - locw
    tot = jnp.sum(c16, axis=0)
    region = -(-(tot + MOE_BLK) // MOE_ROWS) * MOE_ROWS
    base = jnp.cumsum(region) - region
    off = base[None, :] + jnp.cumsum(c16, axis=0) - c16
    ends = jnp.cumsum(region // MOE_ROWS)
    tile_expert = jnp.minimum(jnp.searchsorted(ends, jnp.arange(r_tiles, dtype=jnp.int32), side="right"),
                              N_EXPERTS - 1).astype(jnp.int32)
    flat = lambda a: a.reshape(-1).astype(jnp.int32)
    return flat(off), flat(nblk), flat(loc), tile_expert, ends[-1:].astype(jnp.int32)


def _moe_positions(route, rank, loc_ref, s):
    lane = lax.broadcasted_iota(jnp.int32, route.shape, 1)
    locv = jnp.zeros(route.shape, F32)
    for e in range(N_EXPERTS):
        locv = jnp.where(lane == e, loc_ref[s * N_EXPERTS + e].astype(F32), locv)
    pos = rank + locv
    lanef = lane.astype(F32)
    sel1 = lanef == route[:, ROUTE_I1:ROUTE_I1 + 1]
    sel2 = lanef == route[:, ROUTE_I2:ROUTE_I2 + 1]
    pick = lambda sel, v: jnp.sum(jnp.where(sel, v, 0.0), axis=-1, keepdims=True)
    return pick(sel1, pos), pick(sel2, pos), pick(sel1, route), pick(sel2, route)


def _moe_segment_loops(nblk_ref, s, make_copies):
    for phase in ("start", "wait"):
        for e in range(N_EXPERTS):
            def body(rb, carry, e=e, phase=phase):
                for cp in make_copies(e, rb):
                    getattr(cp, phase)()
                return carry

            lax.fori_loop(0, nblk_ref[s * N_EXPERTS + e], body, 0)


def _moe_compact_kernel(off_ref, nblk_ref, loc_ref, h_ref, route_ref, rank_ref, xz_ref, gz_ref, xs_ref, gs_ref,
                        p_ref, xcat_ref, gcat_ref, sem):
    del xz_ref, gz_ref
    s = pl.program_id(0)
    lp0, lp1, g0, g1 = _moe_positions(route_ref[...], rank_ref[...], loc_ref, s)
    lane = lax.broadcasted_iota(jnp.int32, (MOE_SUB, LANE), 1)
    packed = jnp.where(lane == 0, lp0, jnp.where(lane == 1, lp1, jnp.where(lane == 2, g0, jnp.where(lane == 3, g1, 0.0))))
    pt = packed.T
    lp0r, lp1r, g0r, g1r = pt[0:1], pt[1:2], pt[2:3], pt[3:4]
    for c in range(MOE_CAT // LANE):
        r = (lax.broadcasted_iota(jnp.int32, (LANE, MOE_SUB), 0) + c * LANE).astype(F32)
        m0 = r == lp0r
        m1 = r == lp1r
        p_ref[c * LANE:(c + 1) * LANE, :] = jnp.where(m0, 1.0, jnp.where(m1, 1.0, 0.0)).astype(BF16)
        gate = jnp.sum(jnp.where(m0, g0r, 0.0) + jnp.where(m1, g1r, 0.0), axis=-1, keepdims=True)
        gcat_ref[c * LANE:(c + 1) * LANE, :] = jnp.broadcast_to(gate, (LANE, LANE))
    xcat_ref[...] = jnp.dot(p_ref[...], h_ref[...], preferred_element_type=F32).astype(BF16)

    def copies(e, rb):
        src = pl.multiple_of(loc_ref[s * N_EXPERTS + e] + rb * MOE_BLK, MOE_BLK)
        dst = pl.multiple_of(off_ref[s * N_EXPERTS + e] + rb * MOE_BLK, MOE_ALIGN)
        return (pltpu.make_async_copy(xcat_ref.at[pl.ds(src, MOE_BLK)], xs_ref.at[pl.ds(dst, MOE_BLK)], sem.at[0]),
                pltpu.make_async_copy(gcat_ref.at[pl.ds(src, MOE_BLK)], gs_ref.at[pl.ds(dst, MOE_BLK)], sem.at[1]))

    _moe_segment_loops(nblk_ref, s, copies)


def _moe_uncompact_kernel(off_ref, nblk_ref, loc_ref, route_ref, rank_ref, ys_ref, o_ref, pt_ref, ycat_ref, sem):
    s = pl.program_id(0)

    @pl.when(s == 0)
    def _():
        ycat_ref[...] = jnp.zeros_like(ycat_ref)

    def copies(e, rb):
        src = pl.multiple_of(off_ref[s * N_EXPERTS + e] + rb * MOE_BLK, MOE_ALIGN)
        dst = pl.multiple_of(loc_ref[s * N_EXPERTS + e] + rb * MOE_BLK, MOE_BLK)
        return (pltpu.make_async_copy(ys_ref.at[pl.ds(src, MOE_BLK)], ycat_ref.at[pl.ds(dst, MOE_BLK)], sem.at[0]),)

    _moe_segment_loops(nblk_ref, s, copies)
    lp0, lp1, _, _ = _moe_positions(route_ref[...], rank_ref[...], loc_ref, s)
    for c in range(MOE_CAT // LANE):
        col = (lax.broadcasted_iota(jnp.int32, (MOE_SUB, LANE), 1) + c * LANE).astype(F32)
        pt_ref[:, c * LANE:(c + 1) * LANE] = jnp.where(col == lp0, 1.0, jnp.where(col == lp1, 1.0, 0.0)).astype(BF16)
    o_ref[...] = jnp.dot(pt_ref[...], ycat_ref[...], preferred_element_type=F32)


def _moe_up_kernel(te_ref, nu_ref, x_ref, w1_ref, w3_ref, o_ref):
    del te_ref

    @pl.when(pl.program_id(1) < nu_ref[0])
    def _():
        a = x_ref[...]
        g = jnp.dot(a, w1_ref[0], preferred_element_type=F32)
        u = jnp.dot(a, w3_ref[0], preferred_element_type=F32)
        o_ref[...] = (_silu(g) * u).astype(o_ref.dtype)

    @pl.when(pl.program_id(1) >= nu_ref[0])
    def _():
        o_ref[...] = jnp.zeros_like(o_ref)


def _moe_down_kernel(te_ref, nu_ref, a_ref, w2_ref, g_ref, o_ref):
    del te_ref

    @pl.when(pl.program_id(1) < nu_ref[0])
    def _():
        y = jnp.dot(a_ref[...], w2_ref[0], preferred_element_type=F32)
        o_ref[...] = (g_ref[:, 0:1] * y).astype(o_ref.dtype)

    @pl.when(pl.program_id(1) >= nu_ref[0])
    def _():
        o_ref[...] = jnp.zeros_like(o_ref)


def _moe_sparse(h, route, w1, w3, w2):
    t, k = h.shape
    ne, _, f = w1.shape
    assert t % MOE_SUB == 0 and ne == N_EXPERTS
    n_sub = t // MOE_SUB
    r_tiles = -(-(2 * t + n_sub * ne * (MOE_ALIGN - 1) + ne * (MOE_BLK + MOE_ROWS - 1)) // MOE_ROWS)
    r_rows = r_tiles * MOE_ROWS
    sub = lambda w: pl.BlockSpec((MOE_SUB, w), lambda s, *_: (s, 0))
    any_spec = pl.BlockSpec(memory_space=pl.ANY)

    tri = jnp.asarray(np.tril(np.ones((MOE_SUB, MOE_SUB), np.float32), -1), BF16)
    rank, cnt = pl.pallas_call(
        _moe_rank_kernel, grid=(n_sub,),
        in_specs=[sub(LANE), pl.BlockSpec((MOE_SUB, MOE_SUB), lambda s: (0, 0))],
        out_specs=[sub(LANE), pl.BlockSpec((1, SUBLANE, LANE), lambda s: (s, 0, 0))],
        out_shape=[jax.ShapeDtypeStruct((t, LANE), F32), jax.ShapeDtypeStruct((n_sub, SUBLANE, LANE), F32)],
        compiler_params=_cparams("parallel"), name="moe_rank")(route, tri)
    off, nblk, loc, tile_expert, n_used = _moe_plan(cnt[:, 0, :ne].astype(jnp.int32), r_tiles)

    x_sorted, gate_sorted = pl.pallas_call(
        _moe_compact_kernel,
        grid_spec=pltpu.PrefetchScalarGridSpec(
            num_scalar_prefetch=3, grid=(n_sub,),
            in_specs=[sub(k), sub(LANE), sub(LANE), any_spec, any_spec],
            out_specs=[any_spec, any_spec],
            scratch_shapes=[pltpu.VMEM((MOE_CAT, MOE_SUB), BF16), pltpu.VMEM((MOE_CAT, k), BF16),
                            pltpu.VMEM((MOE_CAT, LANE), F32), pltpu.SemaphoreType.DMA((2,))]),
        out_shape=[jax.ShapeDtypeStruct((r_rows, k), BF16), jax.ShapeDtypeStruct((r_rows, LANE), F32)],
        input_output_aliases={6: 0, 7: 1},
        compiler_params=_cparams("arbitrary"), name="moe_compact",
    )(off, nblk, loc, h, route, rank, jnp.zeros((r_rows, k), BF16), jnp.zeros((r_rows, LANE), F32))

    row_tile = lambda j, i, te, nu: jnp.minimum(i, nu[0] - 1)
    tn = 1024
    act = pl.pallas_call(
        _moe_up_kernel,
        grid_spec=pltpu.PrefetchScalarGridSpec(
            num_scalar_prefetch=2, grid=(f // tn, r_tiles),
            in_specs=[pl.BlockSpec((MOE_ROWS, k), lambda j, i, te, nu: (row_tile(j, i, te, nu), 0)),
                      pl.BlockSpec((1, k, tn), lambda j, i, te, nu: (te[i], 0, j)),
                      pl.BlockSpec((1, k, tn), lambda j, i, te, nu: (te[i], 0, j))],
            out_specs=pl.BlockSpec((MOE_ROWS, tn), lambda j, i, te, nu: (i, j))),
        out_shape=jax.ShapeDtypeStruct((r_rows, f), BF16),
        compiler_params=_cparams("parallel", "arbitrary"), name="moe_up")(tile_expert, n_used, x_sorted, w1, w3)
    y_sorted = pl.pallas_call(
        _moe_down_kernel,
        grid_spec=pltpu.PrefetchScalarGridSpec(
            num_scalar_prefetch=2, grid=(k // tn, r_tiles),
            in_specs=[pl.BlockSpec((MOE_ROWS, f), lambda j, i, te, nu: (row_tile(j, i, te, nu), 0)),
                      pl.BlockSpec((1, f, tn), lambda j, i, te, nu: (te[i], 0, j)),
                      pl.BlockSpec((MOE_ROWS, LANE), lambda j, i, te, nu: (row_tile(j, i, te, nu), 0))],
            out_specs=pl.BlockSpec((MOE_ROWS, tn), lambda j, i, te, nu: (i, j))),
        out_shape=jax.ShapeDtypeStruct((r_rows, k), BF16),
        compiler_params=_cparams("parallel", "arbitrary"), name="moe_down")(tile_expert, n_used, act, w2, gate_sorted)

    return pl.pallas_call(
        _moe_uncompact_kernel,
        grid_spec=pltpu.PrefetchScalarGridSpec(
            num_scalar_prefetch=3, grid=(n_sub,),
            in_specs=[sub(LANE), sub(LANE), any_spec],
            out_specs=sub(k),
            scratch_shapes=[pltpu.VMEM((MOE_SUB, MOE_CAT), BF16), pltpu.VMEM((MOE_CAT, k), BF16),
                            pltpu.SemaphoreType.DMA((1,))]),
        out_shape=jax.ShapeDtypeStruct((t, k), F32),
        compiler_params=_cparams("arbitrary"), name="moe_uncompact")(off, nblk, loc, route, rank, y_sorted)


def _tile_specs(width, col_block, n_tiles):
    per = TT // HALO
    last = n_tiles * per - 1
    cur = pl.BlockSpec((1, TT, width), lambda b, i: (b, i, col_block))
    prev = pl.BlockSpec((1, HALO, width), lambda b, i: (b, jnp.maximum(i * per - 1, 0), col_block))
    nxt = pl.BlockSpec((1, HALO, width), lambda b, i: (b, jnp.minimum((i + 1) * per, last), col_block))
    return cur, prev, nxt


def _segment_edges(i, nct, nt):
    has_prev = jnp.logical_and(i != 0, i != nct)
    has_next = jnp.logical_and(i != nct - 1, i != nt - 1)
    return has_prev, has_next


def _fill_ext(ext_ref, cur, prev, nxt, has_prev, has_next):
    ext_ref[0:HALO, :] = jnp.where(has_prev, prev, 0.0)
    ext_ref[HALO:HALO + TT, :] = cur
    ext_ref[HALO + TT:HALO + TT + HALO, :] = jnp.where(has_next, nxt, 0.0)


def _dwconv_cols(ext_ref, w_ref, b_ref, taps, pad_l, c0, width):
    acc = jnp.broadcast_to(b_ref[:, pl.ds(c0, width)], (TT, width))
    for k in range(taps):
        off = HALO + k - pad_l
        acc = acc + ext_ref[off:off + TT, pl.ds(c0, width)] * w_ref[k:k + 1, pl.ds(c0, width)]
    return acc


def _scan_tile_index(nct, nt, reverse):
    if not reverse:
        return lambda i: i
    return lambda i: jnp.where(i < nct, nct - 1 - i, nt + nct - 1 - i)


CONV_COLS = 256


def _ssd_prep_kernel(xs_ref, xsp_ref, xsn_ref, bc_ref, bcp_ref, bcn_ref, dt_ref, wx_ref, bx_ref, wb_ref, bb_ref,
                     dtb_ref, xo_ref, bo_ref, dto_ref, extx_ref, extb_ref, *, nct, nt):
    i = pl.program_id(1)
    has_prev, has_next = _segment_edges(i, nct, nt)
    _fill_ext(extx_ref, xs_ref[0], xsp_ref[0], xsn_ref[0], has_prev, has_next)
    _fill_ext(extb_ref, bc_ref[0], bcp_ref[0], bcn_ref[0], has_prev, has_next)
    for c0 in range(0, D, CONV_COLS):
        xo_ref[0, :, c0:c0 + CONV_COLS] = _silu(_dwconv_cols(extx_ref, wx_ref, bx_ref, SHORT_CONV, SHORT_PAD_L, c0, CONV_COLS))
    for c0 in range(0, 2 * SSD_GROUPS * SSD_STATE, CONV_COLS):
        bo_ref[0, :, c0:c0 + CONV_COLS] = _silu(_dwconv_cols(extb_ref, wb_ref, bb_ref, SHORT_CONV, SHORT_PAD_L, c0, CONV_COLS))
    dto_ref[0] = _softplus(dt_ref[0] + dtb_ref[...])


def _ssd_prep(proj, conv_w, conv_b, dt_bias, nct, nt):
    nb, tt, _ = proj.shape
    nbc = 2 * SSD_GROUPS * SSD_STATE
    xs_specs = _tile_specs(D, AB_XS // D, nt)
    bc_specs = _tile_specs(nbc, AB_BC // nbc, nt)
    full = lambda shape: pl.BlockSpec(shape, lambda b, i: (0,) * len(shape))
    dtb = jnp.zeros((1, LANE), F32).at[0, :2 * SSD_HEADS].set(dt_bias.reshape(-1))
    return pl.pallas_call(
        functools.partial(_ssd_prep_kernel, nct=nct, nt=nt),
        grid=(nb, nt),
        in_specs=[*xs_specs, *bc_specs,
                  pl.BlockSpec((1, TT, LANE), lambda b, i: (b, i, AB_DT // LANE)),
                  full((SHORT_CONV, D)), full((1, D)), full((SHORT_CONV, nbc)), full((1, nbc)), full((1, LANE))],
        out_specs=[pl.BlockSpec((1, TT, D), lambda b, i: (b, i, 0)),
                   pl.BlockSpec((1, TT, nbc), lambda b, i: (b, i, 0)),
                   pl.BlockSpec((1, TT, LANE), lambda b, i: (b, i, 0))],
        out_shape=[jax.ShapeDtypeStruct((nb, tt, D), F32), jax.ShapeDtypeStruct((nb, tt, nbc), F32),
                   jax.ShapeDtypeStruct((nb, tt, LANE), F32)],
        scratch_shapes=[pltpu.VMEM((TT + 2 * HALO, D), F32), pltpu.VMEM((TT + 2 * HALO, nbc), F32)],
        compiler_params=_cparams("parallel", "arbitrary"), name="ssd_prep",
    )(proj, proj, proj, proj, proj, proj, proj, conv_w[:, :D], conv_b[None, :D], conv_w[:, D:], conv_b[None, D:], dtb)


def _ssd_scan_kernel(*refs, direction, final):
    if final:
        xs_ref, bc_ref, dt_ref, aneg_ref, tri_ref, yin_ref, z_ref, nw_ref, o_ref, h_ref, y_ref = refs
    else:
        xs_ref, bc_ref, dt_ref, aneg_ref, tri_ref, dsk_ref, o_ref, h_ref = refs
    forward = direction == 0

    @pl.when(pl.program_id(1) == 0)
    def _():
        h_ref[...] = jnp.zeros_like(h_ref)

    dt = dt_ref[0]
    d_a = dt * aneg_ref[...]
    a_cs = jnp.dot(tri_ref[...], d_a, preferred_element_type=F32, precision=HIGHEST)
    end = TT - 1 if forward else 0
    tot = a_cs[end:end + 1, :]
    w_end = dt * jnp.exp(tot - a_cs)
    e_in = jnp.exp(a_cs)
    e_tot = jnp.exp(tot)
    a_cs_t, w_end_t, dt_t = a_cs.T, w_end.T, dt.T
    row = lax.broadcasted_iota(jnp.int32, (TT, TT), 0)
    col = lax.broadcasted_iota(jnp.int32, (TT, TT), 1)
    causal = (row >= col) if forward else (row <= col)
    low = col < SSD_HEAD_DIM
    neg = jnp.float32(-jnp.inf)
    ns = SSD_GROUPS * SSD_STATE
    pair_w = 2 * SSD_HEAD_DIM

    for g in range(SSD_GROUPS):
        b_g = bc_ref[0, :, g * SSD_STATE:(g + 1) * SSD_STATE]
        c_g = bc_ref[0, :, ns + g * SSD_STATE:ns + (g + 1) * SSD_STATE].astype(BF16)
        scores = lax.dot_general(c_g, b_g.astype(BF16), (((1,), (1,)), ((), ())), preferred_element_type=F32)
        b_t = b_g.T
        gw = SSD_GROUP_HEADS * SSD_HEAD_DIM
        y_off = jnp.dot(c_g, h_ref[:, g * gw:(g + 1) * gw].astype(BF16), preferred_element_type=F32)
        for pr in range(SSD_GROUP_HEADS // 2):
            head = g * SSD_GROUP_HEADS + 2 * pr
            ca = direction * SSD_HEADS + head
            cb = ca + 1
            c0 = head * SSD_HEAD_DIM
            xs_pair = xs_ref[0, :, c0:c0 + pair_w]
            rhs = jnp.concatenate([jnp.where(low, xs_pair, 0.0), jnp.where(low, 0.0, xs_pair)], axis=0).astype(BF16)

            def head_lhs(c):
                seg = a_cs[:, c:c + 1] - a_cs_t[c:c + 1, :]
                m = scores * jnp.exp(jnp.where(causal, seg, neg)) * dt_t[c:c + 1, :]
                return m, b_t * w_end_t[c:c + 1, :]

            m_a, bw_a = head_lhs(ca)
            m_b, bw_b = head_lhs(cb)
            lhs = jnp.concatenate([jnp.concatenate([m_a, m_b], axis=1),
                                   jnp.concatenate([bw_a, bw_b], axis=1)], axis=0).astype(BF16)
            res = jnp.dot(lhs, rhs, preferred_element_type=F32)
            e_pair = jnp.where(low, e_in[:, ca:ca + 1], e_in[:, cb:cb + 1])
            y_pair = res[:TT] + y_off[:, 2 * pr * SSD_HEAD_DIM:2 * pr * SSD_HEAD_DIM + pair_w] * e_pair
            et_pair = jnp.where(low[0:1], e_tot[:, ca:ca + 1], e_tot[:, cb:cb + 1])
            h_ref[:, c0:c0 + pair_w] = h_ref[:, c0:c0 + pair_w] * et_pair + res[TT:]
            if final:
                y_ref[:, c0:c0 + pair_w] = yin_ref[0, :, c0:c0 + pair_w] + y_pair
            else:
                o_ref[0, :, c0:c0 + pair_w] = xs_pair * dsk_ref[:, c0:c0 + pair_w] + y_pair

    if final:
        u = y_ref[...] * _silu(z_ref[0])
        o_ref[0] = (u * lax.rsqrt(jnp.mean(u * u, axis=-1, keepdims=True) + LN_EPS) * nw_ref[...]).astype(o_ref.dtype)


def _ssd_scan(xs_c, bc_c, dt_sp, a_log, direction, nct, nt, *, d_skip=None, y_in=None, proj=None, norm_w=None):
    nb, tt, _ = xs_c.shape
    nbc = bc_c.shape[-1]
    final = y_in is not None
    tile_of = _scan_tile_index(nct, nt, direction == 1)
    tok = lambda w, cb=0: pl.BlockSpec((1, TT, w), lambda b, i: (b, tile_of(i), cb))
    full = lambda shape: pl.BlockSpec(shape, lambda b, i: (0,) * len(shape))
    aneg = jnp.zeros((1, LANE), F32).at[0, :2 * SSD_HEADS].set(-jnp.exp(a_log.astype(F32)).reshape(-1))
    ones = np.ones((TT, TT), np.float32)
    tri = jnp.asarray(np.tril(ones) if direction == 0 else np.triu(ones))
    args = [xs_c, bc_c, dt_sp, aneg, tri]
    in_specs = [tok(D), tok(nbc), tok(LANE), full((1, LANE)), full((TT, TT))]
    scratch = [pltpu.VMEM((SSD_STATE, D), F32)]
    if final:
        args += [y_in, proj, norm_w[None, :]]
        in_specs += [tok(D), tok(D, AB_Z // D), full((1, D))]
        scratch.append(pltpu.VMEM((TT, D), F32))
        out_dtype = BF16
    else:
        args.append(jnp.repeat(d_skip.astype(F32), SSD_HEAD_DIM)[None, :])
        in_specs.append(full((1, D)))
        out_dtype = F32
    return pl.pallas_call(
        functools.partial(_ssd_scan_kernel, direction=direction, final=final),
        grid=(nb, nt), in_specs=in_specs, out_specs=tok(D),
        out_shape=jax.ShapeDtypeStruct((nb, tt, D), out_dtype), scratch_shapes=scratch,
        compiler_params=_cparams("parallel", "arbitrary"), name=f"ssd_scan{direction}")(*args)


def _rope(x, cos, sin):
    lane = lax.broadcasted_iota(jnp.int32, x.shape, 1)
    quarter = HEAD_DIM // 4
    partner = jnp.where((lane & quarter) == 0, pltpu.roll(x, HEAD_DIM - quarter, 1), pltpu.roll(x, quarter, 1))
    return x * cos + partner * sin


def _attn_kernel(sink_ref, q_ref, kp_ref, kc_ref, kn_ref, vp_ref, vc_ref, vn_ref, kx_ref, vx_ref,
                 cp_ref, cc_ref, cn_ref, sp_ref, sc_ref, sn_ref, o_ref, *, nct, nt):
    g = pl.program_id(1)
    i = pl.program_id(2)
    nt_dims = (((1,), (1,)), ((), ()))
    rows = ATTN_GROUP * TT
    sink_col = jnp.concatenate(
        [jnp.full((TT, 1), sink_ref[g * ATTN_GROUP + j], F32) for j in range(ATTN_GROUP)], axis=0)
    k_ctx = kx_ref[0].astype(BF16)
    v_ctx = vx_ref[0].astype(BF16)

    def finish(o, den):
        o = o / den
        for j in range(ATTN_GROUP):
            o_ref[0, :, j * HEAD_DIM:(j + 1) * HEAD_DIM] = o[j * TT:(j + 1) * TT].astype(o_ref.dtype)

    @pl.when(i >= nct)
    def _():
        cos_c, sin_c = cc_ref[...], sc_ref[...]
        q = jnp.concatenate([_rope(q_ref[0, :, j * HEAD_DIM:(j + 1) * HEAD_DIM], cos_c, sin_c)
                             for j in range(ATTN_GROUP)], axis=0).astype(BF16)
        k_loc = jnp.concatenate([_rope(kp_ref[0], cp_ref[...], sp_ref[...]), _rope(kc_ref[0], cos_c, sin_c),
                                 _rope(kn_ref[0], cn_ref[...], sn_ref[...])], axis=0).astype(BF16)
        v_loc = jnp.concatenate([vp_ref[0], vc_ref[0], vn_ref[0]], axis=0).astype(BF16)
        s_loc = lax.dot_general(q, k_loc, nt_dims, preferred_element_type=F32) * ATTN_SCALE
        a = lax.broadcasted_iota(jnp.int32, (rows, 3 * TT), 0) & (TT - 1)
        j = lax.broadcasted_iota(jnp.int32, (rows, 3 * TT), 1)
        lo = jnp.where(i > nct, 0, TT)
        hi = jnp.where(i < nt - 1, 3 * TT, 2 * TT)
        valid = (j >= a) & (j <= a + 2 * WINDOW) & (j >= lo) & (j < hi)
        s_loc = jnp.where(valid, s_loc, jnp.float32(-jnp.inf))
        s_ctx = lax.dot_general(q, k_ctx, nt_dims, preferred_element_type=F32) * ATTN_SCALE
        m = jnp.maximum(jnp.maximum(jnp.max(s_loc, axis=-1, keepdims=True), jnp.max(s_ctx, axis=-1, keepdims=True)),
                        sink_col)
        p_loc = jnp.exp(s_loc - m)
        p_ctx = jnp.exp(s_ctx - m)
        den = jnp.sum(p_loc, axis=-1, keepdims=True) + jnp.sum(p_ctx, axis=-1, keepdims=True) + jnp.exp(sink_col - m)
        o = (jnp.dot(p_loc.astype(BF16), v_loc, preferred_element_type=F32)
             + jnp.dot(p_ctx.astype(BF16), v_ctx, preferred_element_type=F32))
        finish(o, den)

    @pl.when(i < nct)
    def _():
        q = jnp.concatenate([q_ref[0, :, j * HEAD_DIM:(j + 1) * HEAD_DIM] for j in range(ATTN_GROUP)],
                            axis=0).astype(BF16)
        s_ctx = lax.dot_general(q, k_ctx, nt_dims, preferred_element_type=F32) * ATTN_SCALE
        m = jnp.maximum(jnp.max(s_ctx, axis=-1, keepdims=True), sink_col)
        p_ctx = jnp.exp(s_ctx - m)
        den = jnp.sum(p_ctx, axis=-1, keepdims=True) + jnp.exp(sink_col - m)
        finish(jnp.dot(p_ctx.astype(BF16), v_ctx, preferred_element_type=F32), den)


def _rope_tables(n_ctx, s):
    quarter = HEAD_DIM // 4
    inv_freq = ROPE_BASE ** (-jnp.arange(quarter, dtype=F32) / quarter)
    pos = jnp.arange(s)
    ang_r = (pos // GRID_W).astype(F32)[:, None] * inv_freq[None, :]
    ang_c = (pos % GRID_W).astype(F32)[:, None] * inv_freq[None, :]
    cos = jnp.concatenate([jnp.cos(ang_r), jnp.cos(ang_r), jnp.cos(ang_c), jnp.cos(ang_c)], axis=-1)
    sin = jnp.concatenate([-jnp.sin(ang_r), jnp.sin(ang_r), -jnp.sin(ang_c), jnp.sin(ang_c)], axis=-1)
    pad = jnp.zeros((n_ctx, HEAD_DIM), F32)
    return jnp.concatenate([pad, cos], axis=0), jnp.concatenate([pad, sin], axis=0)


def _attention(proj, sink, cos, sin, n_ctx, nct, nt):
    nb, tt, _ = proj.shape
    qw = ATTN_GROUP * HEAD_DIM
    prev_i = lambda i: jnp.maximum(i - 1, 0)
    next_i = lambda i: jnp.minimum(i + 1, nt - 1)

    def kv(base, which):
        return pl.BlockSpec((1, TT, HEAD_DIM), lambda b, g, i: (b, which(i), base // HEAD_DIM + g))

    def tab(which):
        return pl.BlockSpec((TT, HEAD_DIM), lambda b, g, i: (which(i), 0))

    same = lambda i: i
    in_specs = [pl.BlockSpec(memory_space=pltpu.SMEM),
                pl.BlockSpec((1, TT, qw), lambda b, g, i: (b, i, AB_Q // qw + g)),
                kv(AB_K, prev_i), kv(AB_K, same), kv(AB_K, next_i),
                kv(AB_V, prev_i), kv(AB_V, same), kv(AB_V, next_i),
                pl.BlockSpec((1, n_ctx, HEAD_DIM), lambda b, g, i: (b, 0, AB_K // HEAD_DIM + g)),
                pl.BlockSpec((1, n_ctx, HEAD_DIM), lambda b, g, i: (b, 0, AB_V // HEAD_DIM + g)),
                tab(prev_i), tab(same), tab(next_i), tab(prev_i), tab(same), tab(next_i)]
    return pl.pallas_call(
        functools.partial(_attn_kernel, nct=nct, nt=nt),
        grid=(nb, ATTN_KV, nt), in_specs=in_specs,
        out_specs=pl.BlockSpec((1, TT, qw), lambda b, g, i: (b, i, g)),
        out_shape=jax.ShapeDtypeStruct((nb, tt, D), BF16),
        compiler_params=_cparams("parallel", "parallel", "arbitrary"), name="attention",
    )(sink.astype(F32), proj, proj, proj, proj, proj, proj, proj, proj, proj, cos, cos, cos, sin, sin, sin)


def _conformer_kernel(a_ref, ap_ref, an_ref, b_ref, bp_ref, bn_ref, w_ref, cb_ref, ln_ref, o_ref, ext_ref, acc_ref,
                      *, nct, nt):
    i = pl.program_id(1)
    has_prev, has_next = _segment_edges(i, nct, nt)
    _fill_ext(ext_ref, a_ref[0] * _sigmoid(b_ref[0]), ap_ref[0] * _sigmoid(bp_ref[0]), an_ref[0] * _sigmoid(bn_ref[0]),
              has_prev, has_next)

    def cols(c, carry):
        c0 = pl.multiple_of(c * CONV_COLS, CONV_COLS)
        acc_ref[:, pl.ds(c0, CONV_COLS)] = _dwconv_cols(ext_ref, w_ref, cb_ref, CONF_KERNEL, CONF_PAD_L, c0, CONV_COLS)
        return carry

    lax.fori_loop(0, D // CONV_COLS, cols, 0)
    u = acc_ref[...]
    mu = jnp.mean(u, axis=-1, keepdims=True)
    uc = u - mu
    var = jnp.mean(uc * uc, axis=-1, keepdims=True)
    o_ref[0] = _silu(uc * lax.rsqrt(var + LN_EPS) * ln_ref[0:1, :] + ln_ref[1:2, :]).astype(o_ref.dtype)


def _conformer(proj, dw_w, dw_b, ln_g, ln_b, nct, nt):
    nb, tt, _ = proj.shape
    full = lambda shape: pl.BlockSpec(shape, lambda b, i: (0,) * len(shape))
    return pl.pallas_call(
        functools.partial(_conformer_kernel, nct=nct, nt=nt),
        grid=(nb, nt),
        in_specs=[*_tile_specs(D, CD_GLU_A // D, nt), *_tile_specs(D, CD_GLU_B // D, nt),
                  full((CONF_KERNEL, D)), full((1, D)), full((2, D))],
        out_specs=pl.BlockSpec((1, TT, D), lambda b, i: (b, i, 0)),
        out_shape=jax.ShapeDtypeStruct((nb, tt, D), BF16),
        scratch_shapes=[pltpu.VMEM((TT + 2 * HALO, D), F32), pltpu.VMEM((TT, D), F32)],
        compiler_params=_cparams("parallel", "arbitrary"), name="conformer",
    )(proj, proj, proj, proj, proj, proj, dw_w, dw_b[None, :], jnp.stack([ln_g, ln_b]))


def _rg_prep_kernel(x_ref, xp_ref, xn_ref, w_ref, cb_ref, gw_ref, gb_ref, lam_ref, af_ref, bf_ref, ab_ref, bb_ref,
                    ext_ref, *, nct, nt):
    i = pl.program_id(1)
    has_prev, has_next = _segment_edges(i, nct, nt)
    _fill_ext(ext_ref, x_ref[0], xp_ref[0], xn_ref[0], has_prev, has_next)
    outs = ((af_ref, bf_ref), (ab_ref, bb_ref))
    for blk in range(RG_BLOCKS):
        c0 = blk * RG_BLOCK
        xr = _dwconv_cols(ext_ref, w_ref, cb_ref, SHORT_CONV, SHORT_PAD_L, c0, RG_BLOCK)
        pre = jnp.dot(xr.astype(BF16), gw_ref[blk], preferred_element_type=F32) + gb_ref[blk:blk + 1, :]
        for d in range(2):
            r = _sigmoid(pre[:, (2 * d) * RG_BLOCK:(2 * d + 1) * RG_BLOCK])
            gate_i = _sigmoid(pre[:, (2 * d + 1) * RG_BLOCK:(2 * d + 2) * RG_BLOCK])
            log_a = -RG_C * r * _softplus(-lam_ref[d:d + 1, c0:c0 + RG_BLOCK])
            outs[d][0][0, :, c0:c0 + RG_BLOCK] = jnp.exp(log_a)
            outs[d][1][0, :, c0:c0 + RG_BLOCK] = jnp.sqrt(-_expm1(2.0 * log_a)) * gate_i * xr


def _rg_prep(proj, conv_w, conv_b, gate_w, gate_b, lam, nct, nt):
    nb, tt, _ = proj.shape
    full = lambda shape: pl.BlockSpec(shape, lambda b, i: (0,) * len(shape))
    gw = jnp.transpose(gate_w, (2, 3, 0, 1, 4)).reshape(RG_BLOCKS, RG_BLOCK, 4 * RG_BLOCK).astype(BF16)
    gb = jnp.transpose(gate_b.reshape(2, 2, RG_BLOCKS, RG_BLOCK), (2, 0, 1, 3)).reshape(RG_BLOCKS, 4 * RG_BLOCK)
    tok = pl.BlockSpec((1, TT, D), lambda b, i: (b, i, 0))
    shp = jax.ShapeDtypeStruct((nb, tt, D), F32)
    return pl.pallas_call(
        functools.partial(_rg_prep_kernel, nct=nct, nt=nt),
        grid=(nb, nt),
        in_specs=[*_tile_specs(D, CD_RX // D, nt), full((SHORT_CONV, D)), full((1, D)),
                  full((RG_BLOCKS, RG_BLOCK, 4 * RG_BLOCK)), full((RG_BLOCKS, 4 * RG_BLOCK)), full((2, D))],
        out_specs=[tok, tok, tok, tok], out_shape=[shp, shp, shp, shp],
        scratch_shapes=[pltpu.VMEM((TT + 2 * HALO, D), F32)],
        compiler_params=_cparams("parallel", "arbitrary"), name="rg_prep",
    )(proj, proj, proj, conv_w, conv_b[None, :], gw, gb.astype(F32), lam.astype(F32))


def _rg_scan_kernel(*refs, reverse, final):
    if final:
        a_ref, b_ref, hf_ref, gate_ref, o_ref, h_ref, hs_ref = refs
    else:
        a_ref, b_ref, o_ref, h_ref, hs_ref = refs

    @pl.when(pl.program_id(1) == 0)
    def _():
        h_ref[...] = jnp.zeros_like(h_ref)

    def step(s, h):
        t = (TT - 1 - s) if reverse else s
        h = a_ref[0, pl.ds(t, 1), :] * h + b_ref[0, pl.ds(t, 1), :]
        hs_ref[pl.ds(t, 1), :] = h
        return h

    h_ref[...] = lax.fori_loop(0, TT, step, h_ref[...])
    if final:
        o_ref[0] = ((hf_ref[0] + hs_ref[...]) * _gelu_tanh(gate_ref[0])).astype(o_ref.dtype)
    else:
        o_ref[0] = hs_ref[...]


def _rg_scan(a, b, reverse, nct, nt, *, h_fwd=None, proj=None):
    nb, tt, _ = a.shape
    final = h_fwd is not None
    tile_of = _scan_tile_index(nct, nt, reverse)
    tok = lambda cb=0: pl.BlockSpec((1, TT, D), lambda bb, i: (bb, tile_of(i), cb))
    args, in_specs = [a, b], [tok(), tok()]
    if final:
        args += [h_fwd, proj]
        in_specs += [tok(), tok(CD_RGATE // D)]
    return pl.pallas_call(
        functools.partial(_rg_scan_kernel, reverse=reverse, final=final),
        grid=(nb, nt), in_specs=in_specs, out_specs=tok(),
        out_shape=jax.ShapeDtypeStruct((nb, tt, D), BF16 if final else F32),
        scratch_shapes=[pltpu.VMEM((1, D), F32), pltpu.VMEM((TT, D), F32)],
        compiler_params=_cparams("parallel", "arbitrary"), name=f"rg_scan{int(reverse)}")(*args)


def _even_mixer(h, w_in, w_out, conv_w, conv_b, dt_bias, a_log, d_skip, norm_w, sink, cos, sin, n_ctx, nct, nt):
    nb, tt, _ = h.shape
    w = w_in
    w_perm = jnp.concatenate([w[:, 2048:4096], w[:, 0:2048], w[:, 5184:7232], w[:, 4096:5120], w[:, 7232:7744],
                              w[:, 7744:8256], w[:, 5120:5184], jnp.zeros((D, AB_N - 8256), w.dtype)], axis=1)
    proj = _matmul([h.reshape(nb * tt, D)], [w_perm.astype(BF16)], F32, "ab_in").reshape(nb, tt, AB_N)
    xs_c, bc_c, dt_sp = _ssd_prep(proj, conv_w, conv_b, dt_bias, nct, nt)
    y_fwd = _ssd_scan(xs_c, bc_c, dt_sp, a_log, 0, nct, nt, d_skip=d_skip)
    ssd_o = _ssd_scan(xs_c, bc_c, dt_sp, a_log, 1, nct, nt, y_in=y_fwd, proj=proj, norm_w=norm_w)
    att_o = _attention(proj, sink, cos, sin, n_ctx, nct, nt)
    wo = w_out.astype(BF16)
    return _matmul([ssd_o.reshape(nb * tt, D), att_o.reshape(nb * tt, D)], [wo[:D], wo[D:]], F32, "ab_out")


def _odd_mixer(h, w_in, w_out, dw_w, dw_b, cln_g, cln_b, rconv_w, rconv_b, gate_w, gate_b, lam, nct, nt):
    nb, tt, _ = h.shape
    proj = _matmul([h.reshape(nb * tt, D)], [w_in.astype(BF16)], F32, "cd_in").reshape(nb, tt, CD_N)
    conf_o = _conformer(proj, dw_w, dw_b, cln_g, cln_b, nct, nt)
    a_f, b_f, a_b, b_b = _rg_prep(proj, rconv_w, rconv_b, gate_w, gate_b, lam, nct, nt)
    h_fwd = _rg_scan(a_f, b_f, False, nct, nt)
    rg_o = _rg_scan(a_b, b_b, True, nct, nt, h_fwd=h_fwd, proj=proj)
    wo = w_out.astype(BF16)
    return _matmul([conf_o.reshape(nb * tt, D), rg_o.reshape(nb * tt, D)], [wo[:D], wo[D:]], F32, "cd_out")


def kernel(x, c, ctx, c_ctx, mod_w, mod_b, ln_g, ln_b, ab_w_in, ab_w_out, ssd_conv_w, ssd_conv_b, ssd_dt_bias, ssd_a_log, ssd_d, ssd_norm_w, attn_sink, ffn_w1, ffn_w3, ffn_w2, cd_w_in, cd_w_out, conf_dw_w, conf_dw_b, conf_ln_g, conf_ln_b, rg_conv_w, rg_conv_b, rg_gate_w, rg_gate_b, rg_lambda, moe_router, moe_w1, moe_w3, moe_w2):
    nb, s, _ = x.shape
    n_ctx = ctx.shape[1]
    depth = mod_w.shape[0]
    assert s % TT == 0 and n_ctx % TT == 0 and n_ctx >= TT and x.shape[2] == D
    tt = n_ctx + s
    nct, nt = n_ctx // TT, tt // TT
    alpha = (2 * depth) ** 0.25

    modt = _modulation_tables(c, c_ctx, mod_w, mod_b)
    cos, sin = _rope_tables(n_ctx, s)
    xs = jnp.concatenate([ctx, x], axis=1)
    (h,) = _post(xs, n_ctx, nxt=(modt[0], 0, 1))
    for l in range(depth):
        j = l // 2
        if l % 2 == 0:
            mix = _even_mixer(h, ab_w_in[j], ab_w_out[j], ssd_conv_w[j], ssd_conv_b[j], ssd_dt_bias[j], ssd_a_log[j],
                              ssd_d[j], ssd_norm_w[j], attn_sink[j], cos, sin, n_ctx, nct, nt)
        else:
            mix = _odd_mixer(h, cd_w_in[j], cd_w_out[j], conf_dw_w[j], conf_dw_b[j], conf_ln_g[j], conf_ln_b[j],
                             rg_conv_w[j], rg_conv_b[j], rg_gate_w[j], rg_gate_b[j], rg_lambda[j], nct, nt)
        ln0 = jnp.stack([ln_g[l, 0], ln_b[l, 0]])
        ln1 = jnp.stack([ln_g[l, 1], ln_b[l, 1]])
        res = (mix.reshape(nb, tt, D), modt[l], 2, ln0, alpha)
        hf = h.reshape(nb * tt, D)
        if l % 2 == 0:
            xs, h = _post(xs, n_ctx, res=res, nxt=(modt[l], 3, 4))
            hf = h.reshape(nb * tt, D)
            act = _swiglu_up(hf, ffn_w1[j].astype(BF16), ffn_w3[j].astype(BF16), "ffn_up")
            f = _matmul([act], [ffn_w2[j].astype(BF16)], F32, "ffn_down")
        else:
            wr = jnp.zeros((D, LANE), F32).at[:, :N_EXPERTS].set(moe_router[j])
            xs, h, comb = _post(xs, n_ctx, res=res, nxt=(modt[l], 3, 4), router=wr)
            hf = h.reshape(nb * tt, D)
            f = _moe_sparse(hf, comb.reshape(nb * tt, LANE), moe_w1[j].astype(BF16), moe_w3[j].astype(BF16),
                            moe_w2[j].astype(BF16))
        res = (f.reshape(nb, tt, D), modt[l], 5, ln1, alpha)
        if l + 1 < depth:
            xs, h = _post(xs, n_ctx, res=res, nxt=(modt[l + 1], 0, 1))
        else:
            (xs,) = _post(xs, n_ctx, res=res)
    return xs[:, n_ctx:]
```

```python
import functools
import math

import jax
import jax.numpy as jnp
import numpy as np
from jax import lax
from jax.experimental import pallas as pl
from jax.experimental.pallas import tpu as pltpu

F32 = jnp.float32
BF16 = jnp.bfloat16
HIGHEST = lax.Precision.HIGHEST

LANE = 128
SUBLANE = 8
VMEM_BYTES = 64 * 1024 * 1024
VMEM_LIMIT = VMEM_BYTES * 7 // 8
MATMUL_VMEM_BUDGET = VMEM_BYTES * 5 // 8

D = 2048
GRID_W = 64
SSD_HEAD_DIM = 64
SSD_HEADS = D // SSD_HEAD_DIM
SSD_GROUPS = 4
SSD_STATE = 128
SSD_GROUP_HEADS = SSD_HEADS // SSD_GROUPS
SHORT_CONV = 4
SHORT_PAD_L = 2
HEAD_DIM = 128
ATTN_HEADS = D // HEAD_DIM
ATTN_KV = ATTN_HEADS // 4
ATTN_GROUP = ATTN_HEADS // ATTN_KV
WINDOW = 128
ATTN_SCALE = HEAD_DIM ** -0.5
ROPE_BASE = 10000.0
CONF_KERNEL = 31
CONF_PAD_L = 15
RG_BLOCK = 128
RG_BLOCKS = D // RG_BLOCK
RG_C = 8.0
N_EXPERTS = 8
ROUTE_I1, ROUTE_I2 = N_EXPERTS, N_EXPERTS + 1
LN_EPS = 1e-5

TT = 128
HALO = 16

AB_XS, AB_Z, AB_Q, AB_BC, AB_K, AB_V, AB_DT = 0, 2048, 4096, 6144, 7168, 7680, 8192
AB_N = 8448
CD_GLU_A, CD_GLU_B, CD_RX, CD_RGATE = 0, 2048, 4096, 6144
CD_N = 8192


def _cparams(*sem):
    return pltpu.CompilerParams(dimension_semantics=sem, vmem_limit_bytes=VMEM_LIMIT)


def _sigmoid(x):
    return 1.0 / (1.0 + jnp.exp(-x))


def _silu(x):
    return x * _sigmoid(x)


def _softplus(x):
    return jnp.maximum(x, 0.0) + jnp.log1p(jnp.exp(-jnp.abs(x)))


def _expm1(x):
    u = jnp.exp(x)
    near = jnp.where(u == 1.0, x, (u - 1.0) * x / jnp.log(u))
    return jnp.where(jnp.abs(x) < 0.5, near, u - 1.0)


def _gelu_tanh(x):
    return 0.5 * x * (1.0 + jnp.tanh(math.sqrt(2.0 / math.pi) * (x + 0.044715 * (x * x * x))))


def _mod_kernel(c_ref, w_ref, b_ref, o_ref):
    s = _silu(c_ref[...])
    o_ref[0] = jnp.dot(s, w_ref[0], preferred_element_type=F32, precision=HIGHEST) + b_ref[0]


def _modulation_tables(c, c_ctx, mod_w, mod_b):
    depth, _, n6 = mod_w.shape
    nb = c.shape[0]
    rows = -(-(nb + 1) // SUBLANE) * SUBLANE
    cc = jnp.zeros((rows, D), F32).at[:nb].set(c).at[nb].set(c_ctx)
    tn = 1024
    out = pl.pallas_call(
        _mod_kernel,
        grid=(depth, n6 // tn),
        in_specs=[pl.BlockSpec((rows, D), lambda l, j: (0, 0)),
                  pl.BlockSpec((1, D, tn), lambda l, j: (l, 0, j)),
                  pl.BlockSpec((1, 1, tn), lambda l, j: (l, 0, j))],
        out_specs=pl.BlockSpec((1, rows, tn), lambda l, j: (l, 0, j)),
        out_shape=jax.ShapeDtypeStruct((depth, rows, n6), F32),
        compiler_params=_cparams("arbitrary", "arbitrary"),
        name="modulation",
    )(cc, mod_w, mod_b.reshape(depth, 1, n6))
    lat = out[:, :nb].reshape(depth, nb, 1, 6, D)
    ctx = jnp.broadcast_to(out[:, nb].reshape(depth, 1, 1, 6, D), (depth, nb, 1, 6, D))
    return jnp.concatenate([ctx, lat], axis=2)


def _post_kernel(*refs, alpha, gate_row, mod_rows, has_res, has_h, has_router):
    it = iter(refs)
    x_ref = next(it)
    if has_res:
        m_ref, modg_ref, ln_ref = next(it), next(it), next(it)
    if has_h:
        modn_ref = next(it)
    if has_router:
        wr_ref = next(it)
    if has_res:
        xo_ref = next(it)
    if has_h:
        h_ref = next(it)
    if has_router:
        comb_ref = next(it)

    x = x_ref[0]
    if has_res:
        g = modg_ref[0, 0, gate_row:gate_row + 1, :]
        xf = alpha * x + g * m_ref[0].astype(F32)
        mu = jnp.mean(xf, axis=-1, keepdims=True)
        xc = xf - mu
        var = jnp.mean(xc * xc, axis=-1, keepdims=True)
        x = xc * lax.rsqrt(var + LN_EPS) * ln_ref[0:1, :] + ln_ref[1:2, :]
        xo_ref[0] = x
    if has_h:
        sh = modn_ref[0, 0, mod_rows[0]:mod_rows[0] + 1, :]
        sc = modn_ref[0, 0, mod_rows[1]:mod_rows[1] + 1, :]
        h = x * (1.0 + sc) + sh
        h_ref[0] = h.astype(BF16)
        if has_router:
            logits = jnp.dot(h, wr_ref[...], preferred_element_type=F32, precision=HIGHEST)
            lane = lax.broadcasted_iota(jnp.int32, logits.shape, 1).astype(F32)
            neg = jnp.float32(-jnp.inf)
            lg = jnp.where(lane < N_EXPERTS, logits, neg)
            m1 = jnp.max(lg, axis=-1, keepdims=True)
            i1 = jnp.min(jnp.where(lg == m1, lane, float(LANE)), axis=-1, keepdims=True)
            lg2 = jnp.where(lane == i1, neg, lg)
            m2 = jnp.max(lg2, axis=-1, keepdims=True)
            i2 = jnp.min(jnp.where(lg2 == m2, lane, float(LANE)), axis=-1, keepdims=True)
            e2 = jnp.exp(m2 - m1)
            den = 1.0 + e2
            gates = jnp.where(lane == i1, 1.0 / den, 0.0) + jnp.where(lane == i2, e2 / den, 0.0)
            comb_ref[0] = jnp.where(lane == ROUTE_I1, i1, jnp.where(lane == ROUTE_I2, i2, gates))


def _post(x, n_ctx, *, res=None, nxt=None, router=None, latent_only=False):
    nb, tt, _ = x.shape
    tm = 256 if n_ctx % 256 == 0 else TT
    nct = n_ctx // tm
    first = nct if latent_only else 0
    tt_out = tt - first * tm
    tok_in = pl.BlockSpec((1, tm, D), lambda b, i: (b, i + first, 0))
    tok = pl.BlockSpec((1, tm, D), lambda b, i: (b, i, 0))
    modspec = pl.BlockSpec((1, 1, 6, D), lambda b, i: (b, jnp.where(i + first >= nct, 1, 0), 0, 0))
    args, in_specs, out_shape, out_specs = [x], [tok_in], [], []
    if res is not None:
        m, modt, gate_row, ln_gb, alpha = res
        args += [m, modt, ln_gb]
        in_specs += [tok_in, modspec, pl.BlockSpec((2, D), lambda b, i: (0, 0))]
        out_shape.append(jax.ShapeDtypeStruct((nb, tt_out, D), F32))
        out_specs.append(tok)
    else:
        gate_row, alpha = 0, 1.0
    if nxt is not None:
        args.append(nxt[0])
        in_specs.append(modspec)
        out_shape.append(jax.ShapeDtypeStruct((nb, tt_out, D), BF16))
        out_specs.append(tok)
    if router is not None:
        args.append(router)
        in_specs.append(pl.BlockSpec((D, LANE), lambda b, i: (0, 0)))
        out_shape.append(jax.ShapeDtypeStruct((nb, tt_out, LANE), F32))
        out_specs.append(pl.BlockSpec((1, tm, LANE), lambda b, i: (b, i, 0)))
    kern = functools.partial(_post_kernel, alpha=alpha, gate_row=gate_row,
                             mod_rows=(nxt[1], nxt[2]) if nxt is not None else (0, 0),
                             has_res=res is not None, has_h=nxt is not None, has_router=router is not None)
    return pl.pallas_call(kern, grid=(nb, tt_out // tm), in_specs=in_specs, out_specs=out_specs, out_shape=out_shape,
                          compiler_params=_cparams("parallel", "arbitrary"), name="post")(*args)


def _mm_kernel(*refs, n_a):
    o_ref = refs[-1]
    acc = jnp.dot(refs[0][...], refs[n_a][...], preferred_element_type=F32)
    for t in range(1, n_a):
        acc = acc + jnp.dot(refs[t][...], refs[n_a + t][...], preferred_element_type=F32)
    o_ref[...] = acc.astype(o_ref.dtype)


def _glu_kernel(a_ref, w1_ref, w3_ref, o_ref):
    a = a_ref[...]
    g = jnp.dot(a, w1_ref[...], preferred_element_type=F32)
    u = jnp.dot(a, w3_ref[...], preferred_element_type=F32)
    o_ref[...] = (_silu(g) * u).astype(o_ref.dtype)


def _pick_tiles(m, n, k_total, n_w, out_bytes):
    best = None
    for tm in (1024, 512, 256, 128):
        if m % tm:
            continue
        for tn in (1024, 768, 512, 256, 128):
            if n % tn:
                continue
            need = 2 * (tm * k_total * 2 + k_total * tn * 2 * n_w + tm * tn * out_bytes)
            if need <= MATMUL_VMEM_BUDGET and (best is None or tm * tn > best[0] * best[1]):
                best = (tm, tn)
    assert best is not None, (m, n, k_total)
    return best


def _matmul(a_list, w_list, out_dtype, name):
    m, n = a_list[0].shape[0], w_list[0].shape[1]
    k_total = sum(a.shape[1] for a in a_list)
    tm, tn = _pick_tiles(m, n, k_total, 1, jnp.dtype(out_dtype).itemsize)
    in_specs = ([pl.BlockSpec((tm, a.shape[1]), lambda j, i: (i, 0)) for a in a_list]
                + [pl.BlockSpec((w.shape[0], tn), lambda j, i: (0, j)) for w in w_list])
    return pl.pallas_call(
        functools.partial(_mm_kernel, n_a=len(a_list)),
        grid=(n // tn, m // tm), in_specs=in_specs,
        out_specs=pl.BlockSpec((tm, tn), lambda j, i: (i, j)),
        out_shape=jax.ShapeDtypeStruct((m, n), out_dtype),
        compiler_params=_cparams("parallel", "arbitrary"), name=name)(*a_list, *w_list)


def _swiglu_up(a, w1, w3, name):
    m, k = a.shape
    n = w1.shape[1]
    tm, tn = _pick_tiles(m, n, k, 2, 2)
    return pl.pallas_call(
        _glu_kernel, grid=(n // tn, m // tm),
        in_specs=[pl.BlockSpec((tm, k), lambda j, i: (i, 0)),
                  pl.BlockSpec((k, tn), lambda j, i: (0, j)),
                  pl.BlockSpec((k, tn), lambda j, i: (0, j))],
        out_specs=pl.BlockSpec((tm, tn), lambda j, i: (i, j)),
        out_shape=jax.ShapeDtypeStruct((m, n), BF16),
        compiler_params=_cparams("parallel", "arbitrary"), name=name)(a, w1, w3)


MOE_SUB = 512
MOE_ROWS = 512
MOE_ALIGN = 16
MOE_BLK = 64
MOE_CAT = -(-(2 * MOE_SUB + N_EXPERTS * (MOE_ALIGN - 1 + MOE_BLK - 1)) // LANE) * LANE


def _moe_rank_kernel(route_ref, tri_ref, rank_ref, cnt_ref):
    route = route_ref[...]
    lane = lax.broadcasted_iota(jnp.int32, route.shape, 1).astype(F32)
    onehot = jnp.where(lane == route[:, ROUTE_I1:ROUTE_I1 + 1], 1.0,
                       jnp.where(lane == route[:, ROUTE_I2:ROUTE_I2 + 1], 1.0, 0.0))
    rank_ref[...] = jnp.dot(tri_ref[...], onehot.astype(BF16), preferred_element_type=F32)
    cnt_ref[0] = jnp.broadcast_to(jnp.sum(onehot, axis=0, keepdims=True), (SUBLANE, LANE))


def _moe_plan(cnt, r_tiles):
    c16 = -(-cnt // MOE_ALIGN) * MOE_ALIGN
    nblk = -(-c16 // MOE_BLK)
    locw = nblk * MOE_BLK
    loc = jnp.cumsum(locw, axis=1) - locw
    tot = jnp.sum(c16, axis=0)
    region = -(-(tot + MOE_BLK) // MOE_ROWS) * MOE_ROWS
    base = jnp.cumsum(region) - region
    off = base[None, :] + jnp.cumsum(c16, axis=0) - c16
    ends = jnp.cumsum(region // MOE_ROWS)
    tile_expert = jnp.minimum(jnp.searchsorted(ends, jnp.arange(r_tiles, dtype=jnp.int32), side="right"),
                              N_EXPERTS - 1).astype(jnp.int32)
    flat = lambda a: a.reshape(-1).astype(jnp.int32)
    return flat(off), flat(nblk), flat(loc), tile_expert, ends[-1:].astype(jnp.int32)


def _moe_positions(route, rank, loc_ref, s):
    lane = lax.broadcasted_iota(jnp.int32, route.shape, 1)
    locv = jnp.zeros(route.shape, F32)
    for e in range(N_EXPERTS):
        locv = jnp.where(lane == e, loc_ref[s * N_EXPERTS + e].astype(F32), locv)
    pos = rank + locv
    lanef = lane.astype(F32)
    sel1 = lanef == route[:, ROUTE_I1:ROUTE_I1 + 1]
    sel2 = lanef == route[:, ROUTE_I2:ROUTE_I2 + 1]
    pick = lambda sel, v: jnp.sum(jnp.where(sel, v, 0.0), axis=-1, keepdims=True)
    return pick(sel1, pos), pick(sel2, pos), pick(sel1, route), pick(sel2, route)


def _moe_segment_loops(nblk_ref, s, make_copies):
    for phase in ("start", "wait"):
        for e in range(N_EXPERTS):
            def body(rb, carry, e=e, phase=phase):
                for cp in make_copies(e, rb):
                    getattr(cp, phase)()
                return carry

            lax.fori_loop(0, nblk_ref[s * N_EXPERTS + e], body, 0)


def _moe_compact_kernel(off_ref, nblk_ref, loc_ref, h_ref, route_ref, rank_ref, xz_ref, gz_ref, xs_ref, gs_ref,
                        p_ref, xcat_ref, gcat_ref, sem):
    del xz_ref, gz_ref
    s = pl.program_id(0)
    lp0, lp1, g0, g1 = _moe_positions(route_ref[...], rank_ref[...], loc_ref, s)
    lane = lax.broadcasted_iota(jnp.int32, (MOE_SUB, LANE), 1)
    packed = jnp.where(lane == 0, lp0, jnp.where(lane == 1, lp1, jnp.where(lane == 2, g0, jnp.where(lane == 3, g1, 0.0))))
    pt = packed.T
    lp0r, lp1r, g0r, g1r = pt[0:1], pt[1:2], pt[2:3], pt[3:4]
    for c in range(MOE_CAT // LANE):
        r = (lax.broadcasted_iota(jnp.int32, (LANE, MOE_SUB), 0) + c * LANE).astype(F32)
        m0 = r == lp0r
        m1 = r == lp1r
        p_ref[c * LANE:(c + 1) * LANE, :] = jnp.where(m0, 1.0, jnp.where(m1, 1.0, 0.0)).astype(BF16)
        gate = jnp.sum(jnp.where(m0, g0r, 0.0) + jnp.where(m1, g1r, 0.0), axis=-1, keepdims=True)
        gcat_ref[c * LANE:(c + 1) * LANE, :] = jnp.broadcast_to(gate, (LANE, LANE))
    xcat_ref[...] = jnp.dot(p_ref[...], h_ref[...], preferred_element_type=F32).astype(BF16)

    def copies(e, rb):
        src = pl.multiple_of(loc_ref[s * N_EXPERTS + e] + rb * MOE_BLK, MOE_BLK)
        dst = pl.multiple_of(off_ref[s * N_EXPERTS + e] + rb * MOE_BLK, MOE_ALIGN)
        return (pltpu.make_async_copy(xcat_ref.at[pl.ds(src, MOE_BLK)], xs_ref.at[pl.ds(dst, MOE_BLK)], sem.at[0]),
                pltpu.make_async_copy(gcat_ref.at[pl.ds(src, MOE_BLK)], gs_ref.at[pl.ds(dst, MOE_BLK)], sem.at[1]))

    _moe_segment_loops(nblk_ref, s, copies)


def _moe_uncompact_kernel(off_ref, nblk_ref, loc_ref, route_ref, rank_ref, ys_ref, o_ref, pt_ref, ycat_ref, sem):
    s = pl.program_id(0)

    @pl.when(s == 0)
    def _():
        ycat_ref[...] = jnp.zeros_like(ycat_ref)

    def copies(e, rb):
        src = pl.multiple_of(off_ref[s * N_EXPERTS + e] + rb * MOE_BLK, MOE_ALIGN)
        dst = pl.multiple_of(loc_ref[s * N_EXPERTS + e] + rb * MOE_BLK, MOE_BLK)
        return (pltpu.make_async_copy(ys_ref.at[pl.ds(src, MOE_BLK)], ycat_ref.at[pl.ds(dst, MOE_BLK)], sem.at[0]),)

    _moe_segment_loops(nblk_ref, s, copies)
    lp0, lp1, _, _ = _moe_positions(route_ref[...], rank_ref[...], loc_ref, s)
    for c in range(MOE_CAT // LANE):
        col = (lax.broadcasted_iota(jnp.int32, (MOE_SUB, LANE), 1) + c * LANE).astype(F32)
        pt_ref[:, c * LANE:(c + 1) * LANE] = jnp.where(col == lp0, 1.0, jnp.where(col == lp1, 1.0, 0.0)).astype(BF16)
    o_ref[...] = jnp.dot(pt_ref[...], ycat_ref[...], preferred_element_type=F32).astype(o_ref.dtype)


def _expert_block_is_new(te_ref, i):
    return jnp.logical_or(i == 0, te_ref[i] != te_ref[jnp.maximum(i - 1, 0)])


def _moe_up_kernel(te_ref, nu_ref, x_ref, w1_ref, w3_ref, o_ref, w1b_ref, w3b_ref):
    i = pl.program_id(1)

    @pl.when(_expert_block_is_new(te_ref, i))
    def _():
        w1b_ref[...] = w1_ref[0].astype(BF16)
        w3b_ref[...] = w3_ref[0].astype(BF16)

    @pl.when(i < nu_ref[0])
    def _():
        a = x_ref[...]
        g = jnp.dot(a, w1b_ref[...], preferred_element_type=F32)
        u = jnp.dot(a, w3b_ref[...], preferred_element_type=F32)
        o_ref[...] = (_silu(g) * u).astype(o_ref.dtype)

    @pl.when(i >= nu_ref[0])
    def _():
        o_ref[...] = jnp.zeros_like(o_ref)


def _moe_down_kernel(te_ref, nu_ref, a_ref, w2_ref, g_ref, o_ref, w2b_ref):
    i = pl.program_id(1)

    @pl.when(_expert_block_is_new(te_ref, i))
    def _():
        w2b_ref[...] = w2_ref[0].astype(BF16)

    @pl.when(i < nu_ref[0])
    def _():
        y = jnp.dot(a_ref[...], w2b_ref[...], preferred_element_type=F32)
        o_ref[...] = (g_ref[:, 0:1] * y).astype(o_ref.dtype)

    @pl.when(i >= nu_ref[0])
    def _():
        o_ref[...] = jnp.zeros_like(o_ref)


def _moe_sparse(h, route, w1, w3, w2):
    t, k = h.shape
    ne, _, f = w1.shape
    assert t % MOE_SUB == 0 and ne == N_EXPERTS
    n_sub = t // MOE_SUB
    r_tiles = -(-(2 * t + n_sub * ne * (MOE_ALIGN - 1) + ne * (MOE_BLK + MOE_ROWS - 1)) // MOE_ROWS)
    r_rows = r_tiles * MOE_ROWS
    sub = lambda w: pl.BlockSpec((MOE_SUB, w), lambda s, *_: (s, 0))
    any_spec = pl.BlockSpec(memory_space=pl.ANY)

    tri = jnp.asarray(np.tril(np.ones((MOE_SUB, MOE_SUB), np.float32), -1), BF16)
    rank, cnt = pl.pallas_call(
        _moe_rank_kernel, grid=(n_sub,),
        in_specs=[sub(LANE), pl.BlockSpec((MOE_SUB, MOE_SUB), lambda s: (0, 0))],
        out_specs=[sub(LANE), pl.BlockSpec((1, SUBLANE, LANE), lambda s: (s, 0, 0))],
        out_shape=[jax.ShapeDtypeStruct((t, LANE), F32), jax.ShapeDtypeStruct((n_sub, SUBLANE, LANE), F32)],
        compiler_params=_cparams("parallel"), name="moe_rank")(route, tri)
    off, nblk, loc, tile_expert, n_used = _moe_plan(cnt[:, 0, :ne].astype(jnp.int32), r_tiles)

    x_sorted, gate_sorted = pl.pallas_call(
        _moe_compact_kernel,
        grid_spec=pltpu.PrefetchScalarGridSpec(
            num_scalar_prefetch=3, grid=(n_sub,),
            in_specs=[sub(k), sub(LANE), sub(LANE), any_spec, any_spec],
            out_specs=[any_spec, any_spec],
            scratch_shapes=[pltpu.VMEM((MOE_CAT, MOE_SUB), BF16), pltpu.VMEM((MOE_CAT, k), BF16),
                            pltpu.VMEM((MOE_CAT, LANE), F32), pltpu.SemaphoreType.DMA((2,))]),
        out_shape=[jax.ShapeDtypeStruct((r_rows, k), BF16), jax.ShapeDtypeStruct((r_rows, LANE), F32)],
        input_output_aliases={6: 0, 7: 1},
        compiler_params=_cparams("arbitrary"), name="moe_compact",
    )(off, nblk, loc, h, route, rank, jnp.zeros((r_rows, k), BF16), jnp.zeros((r_rows, LANE), F32))

    row_tile = lambda j, i, te, nu: jnp.minimum(i, nu[0] - 1)
    tn = 1024
    act = pl.pallas_call(
        _moe_up_kernel,
        grid_spec=pltpu.PrefetchScalarGridSpec(
            num_scalar_prefetch=2, grid=(f // tn, r_tiles),
            in_specs=[pl.BlockSpec((MOE_ROWS, k), lambda j, i, te, nu: (row_tile(j, i, te, nu), 0)),
                      pl.BlockSpec((1, k, tn), lambda j, i, te, nu: (te[i], 0, j)),
                      pl.BlockSpec((1, k, tn), lambda j, i, te, nu: (te[i], 0, j))],
            out_specs=pl.BlockSpec((MOE_ROWS, tn), lambda j, i, te, nu: (i, j)),
            scratch_shapes=[pltpu.VMEM((k, tn), BF16), pltpu.VMEM((k, tn), BF16)]),
        out_shape=jax.ShapeDtypeStruct((r_rows, f), BF16),
        compiler_params=_cparams("parallel", "arbitrary"), name="moe_up")(tile_expert, n_used, x_sorted, w1, w3)
    y_sorted = pl.pallas_call(
        _moe_down_kernel,
        grid_spec=pltpu.PrefetchScalarGridSpec(
            num_scalar_prefetch=2, grid=(k // tn, r_tiles),
            in_specs=[pl.BlockSpec((MOE_ROWS, f), lambda j, i, te, nu: (row_tile(j, i, te, nu), 0)),
                      pl.BlockSpec((1, f, tn), lambda j, i, te, nu: (te[i], 0, j)),
                      pl.BlockSpec((MOE_ROWS, LANE), lambda j, i, te, nu: (row_tile(j, i, te, nu), 0))],
            out_specs=pl.BlockSpec((MOE_ROWS, tn), lambda j, i, te, nu: (i, j)),
            scratch_shapes=[pltpu.VMEM((f, tn), BF16)]),
        out_shape=jax.ShapeDtypeStruct((r_rows, k), BF16),
        compiler_params=_cparams("parallel", "arbitrary"), name="moe_down")(tile_expert, n_used, act, w2, gate_sorted)

    return pl.pallas_call(
        _moe_uncompact_kernel,
        grid_spec=pltpu.PrefetchScalarGridSpec(
            num_scalar_prefetch=3, grid=(n_sub,),
            in_specs=[sub(LANE), sub(LANE), any_spec],
            out_specs=sub(k),
            scratch_shapes=[pltpu.VMEM((MOE_SUB, MOE_CAT), BF16), pltpu.VMEM((MOE_CAT, k), BF16),
                            pltpu.SemaphoreType.DMA((1,))]),
        out_shape=jax.ShapeDtypeStruct((t, k), BF16),
        compiler_params=_cparams("arbitrary"), name="moe_uncompact")(off, nblk, loc, route, rank, y_sorted)


def _tile_specs(width, col_block, n_tiles, tile_of=lambda i: i):
    per = TT // HALO
    last = n_tiles * per - 1
    cur = pl.BlockSpec((1, TT, width), lambda b, i: (b, tile_of(i), col_block))
    prev = pl.BlockSpec((1, HALO, width), lambda b, i: (b, jnp.maximum(tile_of(i) * per - 1, 0), col_block))
    nxt = pl.BlockSpec((1, HALO, width), lambda b, i: (b, jnp.minimum((tile_of(i) + 1) * per, last), col_block))
    return cur, prev, nxt


def _segment_edges(i, nct, nt):
    has_prev = jnp.logical_and(i != 0, i != nct)
    has_next = jnp.logical_and(i != nct - 1, i != nt - 1)
    return has_prev, has_next


def _fill_ext(ext_ref, cur, prev, nxt, has_prev, has_next):
    ext_ref[0:HALO, :] = jnp.where(has_prev, prev, 0.0)
    ext_ref[HALO:HALO + TT, :] = cur
    ext_ref[HALO + TT:HALO + TT + HALO, :] = jnp.where(has_next, nxt, 0.0)


def _dwconv_cols(ext_ref, w_ref, b_ref, taps, pad_l, c0, width):
    acc = jnp.broadcast_to(b_ref[:, pl.ds(c0, width)], (TT, width))
    for k in range(taps):
        off = HALO + k - pad_l
        acc = acc + ext_ref[off:off + TT, pl.ds(c0, width)] * w_ref[k:k + 1, pl.ds(c0, width)]
    return acc


def _dwconv_cols_grouped(ext_ref, w_ref, b_ref, taps, pad_l, c0, width):
    acc = jnp.broadcast_to(b_ref[:, pl.ds(c0, width)], (TT, width))
    rows = TT + SUBLANE
    for r in range(SUBLANE):
        part = None
        for k in range(taps):
            off = HALO + k - pad_l
            if off % SUBLANE != r:
                continue
            term = ext_ref[off - r:off - r + rows, pl.ds(c0, width)] * w_ref[k:k + 1, pl.ds(c0, width)]
            part = term if part is None else part + term
        if part is not None:
            acc = acc + part[r:r + TT]
    return acc


def _scan_tile_index(nct, nt, reverse):
    if not reverse:
        return lambda i: i
    return lambda i: jnp.where(i < nct, nct - 1 - i, nt + nct - 1 - i)


CONV_COLS = 256


def _ssd_prep_kernel(xs_ref, xsp_ref, xsn_ref, bc_ref, bcp_ref, bcn_ref, dt_ref, wx_ref, bx_ref, wb_ref, bb_ref,
                     dtb_ref, xo_ref, bo_ref, dto_ref, extx_ref, extb_ref, *, nct, nt):
    i = pl.program_id(1)
    has_prev, has_next = _segment_edges(i, nct, nt)
    _fill_ext(extx_ref, xs_ref[0], xsp_ref[0], xsn_ref[0], has_prev, has_next)
    _fill_ext(extb_ref, bc_ref[0], bcp_ref[0], bcn_ref[0], has_prev, has_next)
    for c0 in range(0, D, CONV_COLS):
        xo_ref[0, :, c0:c0 + CONV_COLS] = _silu(_dwconv_cols(extx_ref, wx_ref, bx_ref, SHORT_CONV, SHORT_PAD_L, c0, CONV_COLS))
    for c0 in range(0, 2 * SSD_GROUPS * SSD_STATE, CONV_COLS):
        bo_ref[0, :, c0:c0 + CONV_COLS] = _silu(_dwconv_cols(extb_ref, wb_ref, bb_ref, SHORT_CONV, SHORT_PAD_L, c0, CONV_COLS))
    dto_ref[0] = _softplus(dt_ref[0] + dtb_ref[...])


def _ssd_prep(proj, conv_w, conv_b, dt_bias, nct, nt):
    nb, tt, _ = proj.shape
    nbc = 2 * SSD_GROUPS * SSD_STATE
    xs_specs = _tile_specs(D, AB_XS // D, nt)
    bc_specs = _tile_specs(nbc, AB_BC // nbc, nt)
    full = lambda shape: pl.BlockSpec(shape, lambda b, i: (0,) * len(shape))
    dtb = jnp.zeros((1, LANE), F32).at[0, :2 * SSD_HEADS].set(dt_bias.reshape(-1))
    return pl.pallas_call(
        functools.partial(_ssd_prep_kernel, nct=nct, nt=nt),
        grid=(nb, nt),
        in_specs=[*xs_specs, *bc_specs,
                  pl.BlockSpec((1, TT, LANE), lambda b, i: (b, i, AB_DT // LANE)),
                  full((SHORT_CONV, D)), full((1, D)), full((SHORT_CONV, nbc)), full((1, nbc)), full((1, LANE))],
        out_specs=[pl.BlockSpec((1, TT, D), lambda b, i: (b, i, 0)),
                   pl.BlockSpec((1, TT, nbc), lambda b, i: (b, i, 0)),
                   pl.BlockSpec((1, TT, LANE), lambda b, i: (b, i, 0))],
        out_shape=[jax.ShapeDtypeStruct((nb, tt, D), F32), jax.ShapeDtypeStruct((nb, tt, nbc), F32),
                   jax.ShapeDtypeStruct((nb, tt, LANE), F32)],
        scratch_shapes=[pltpu.VMEM((TT + 2 * HALO, D), F32), pltpu.VMEM((TT + 2 * HALO, nbc), F32)],
        compiler_params=_cparams("parallel", "arbitrary"), name="ssd_prep",
    )(proj, proj, proj, proj, proj, proj, proj, conv_w[:, :D], conv_b[None, :D], conv_w[:, D:], conv_b[None, D:], dtb)


def _ssd_scan_kernel(*refs, direction, final):
    if final:
        xs_ref, bc_ref, dt_ref, aneg_ref, tri_ref, yin_ref, z_ref, nw_ref, o_ref, h_ref, y_ref = refs
    else:
        xs_ref, bc_ref, dt_ref, aneg_ref, tri_ref, dsk_ref, o_ref, h_ref = refs
    forward = direction == 0

    @pl.when(pl.program_id(1) == 0)
    def _():
        h_ref[...] = jnp.zeros_like(h_ref)

    dt = dt_ref[0]
    d_a = dt * aneg_ref[...]
    a_cs = jnp.dot(tri_ref[...], d_a, preferred_element_type=F32, precision=HIGHEST)
    end = TT - 1 if forward else 0
    tot = a_cs[end:end + 1, :]
    w_end = dt * jnp.exp(tot - a_cs)
    e_in = jnp.exp(a_cs)
    e_tot = jnp.exp(tot)
    a_cs_t, w_end_t, dt_t = a_cs.T, w_end.T, dt.T
    row = lax.broadcasted_iota(jnp.int32, (TT, TT), 0)
    col = lax.broadcasted_iota(jnp.int32, (TT, TT), 1)
    causal = (row >= col) if forward else (row <= col)
    low = col < SSD_HEAD_DIM
    neg = jnp.float32(-jnp.inf)
    ns = SSD_GROUPS * SSD_STATE
    pair_w = 2 * SSD_HEAD_DIM

    for g in range(SSD_GROUPS):
        b_g = bc_ref[0, :, g * SSD_STATE:(g + 1) * SSD_STATE]
        c_g = bc_ref[0, :, ns + g * SSD_STATE:ns + (g + 1) * SSD_STATE].astype(BF16)
        scores = lax.dot_general(c_g, b_g.astype(BF16), (((1,), (1,)), ((), ())), preferred_element_type=F32)
        b_t = b_g.T
        gw = SSD_GROUP_HEADS * SSD_HEAD_DIM
        y_off = jnp.dot(c_g, h_ref[:, g * gw:(g + 1) * gw].astype(BF16), preferred_element_type=F32)
        for pr in range(SSD_GROUP_HEADS // 2):
            head = g * SSD_GROUP_HEADS + 2 * pr
            ca = direction * SSD_HEADS + head
            cb = ca + 1
            c0 = head * SSD_HEAD_DIM
            xs_pair = xs_ref[0, :, c0:c0 + pair_w]
            rhs = jnp.concatenate([jnp.where(low, xs_pair, 0.0), jnp.where(low, 0.0, xs_pair)], axis=0).astype(BF16)

            def head_lhs(c):
                seg = a_cs[:, c:c + 1] - a_cs_t[c:c + 1, :]
                m = scores * jnp.exp(jnp.where(causal, seg, neg)) * dt_t[c:c + 1, :]
                return m, b_t * w_end_t[c:c + 1, :]

            m_a, bw_a = head_lhs(ca)
            m_b, bw_b = head_lhs(cb)
            lhs = jnp.concatenate([jnp.concatenate([m_a, m_b], axis=1),
                                   jnp.concatenate([bw_a, bw_b], axis=1)], axis=0).astype(BF16)
            res = jnp.dot(lhs, rhs, preferred_element_type=F32)
            e_pair = jnp.where(low, e_in[:, ca:ca + 1], e_in[:, cb:cb + 1])
            y_pair = res[:TT] + y_off[:, 2 * pr * SSD_HEAD_DIM:2 * pr * SSD_HEAD_DIM + pair_w] * e_pair
            et_pair = jnp.where(low[0:1], e_tot[:, ca:ca + 1], e_tot[:, cb:cb + 1])
            h_ref[:, c0:c0 + pair_w] = h_ref[:, c0:c0 + pair_w] * et_pair + res[TT:]
            if final:
                y_ref[:, c0:c0 + pair_w] = yin_ref[0, :, c0:c0 + pair_w] + y_pair
            else:
                o_ref[0, :, c0:c0 + pair_w] = xs_pair * dsk_ref[:, c0:c0 + pair_w] + y_pair

    if final:
        u = y_ref[...] * _silu(z_ref[0])
        o_ref[0] = (u * lax.rsqrt(jnp.mean(u * u, axis=-1, keepdims=True) + LN_EPS) * nw_ref[...]).astype(o_ref.dtype)


def _ssd_scan(xs_c, bc_c, dt_sp, a_log, direction, nct, nt, *, d_skip=None, y_in=None, proj=None, norm_w=None):
    nb, tt, _ = xs_c.shape
    nbc = bc_c.shape[-1]
    final = y_in is not None
    tile_of = _scan_tile_index(nct, nt, direction == 1)
    tok = lambda w, cb=0: pl.BlockSpec((1, TT, w), lambda b, i: (b, tile_of(i), cb))
    full = lambda shape: pl.BlockSpec(shape, lambda b, i: (0,) * len(shape))
    aneg = jnp.zeros((1, LANE), F32).at[0, :2 * SSD_HEADS].set(-jnp.exp(a_log.astype(F32)).reshape(-1))
    ones = np.ones((TT, TT), np.float32)
    tri = jnp.asarray(np.tril(ones) if direction == 0 else np.triu(ones))
    args = [xs_c, bc_c, dt_sp, aneg, tri]
    in_specs = [tok(D), tok(nbc), tok(LANE), full((1, LANE)), full((TT, TT))]
    scratch = [pltpu.VMEM((SSD_STATE, D), F32)]
    if final:
        args += [y_in, proj, norm_w[None, :]]
        in_specs += [tok(D), tok(D, AB_Z // D), full((1, D))]
        scratch.append(pltpu.VMEM((TT, D), F32))
        out_dtype = BF16
    else:
        args.append(jnp.repeat(d_skip.astype(F32), SSD_HEAD_DIM)[None, :])
        in_specs.append(full((1, D)))
        out_dtype = F32
    return pl.pallas_call(
        functools.partial(_ssd_scan_kernel, direction=direction, final=final),
        grid=(nb, nt), in_specs=in_specs, out_specs=tok(D),
        out_shape=jax.ShapeDtypeStruct((nb, tt, D), out_dtype), scratch_shapes=scratch,
        compiler_params=_cparams("parallel", "arbitrary"), name=f"ssd_scan{direction}")(*args)


def _rope(x, cos, sin):
    lane = lax.broadcasted_iota(jnp.int32, x.shape, 1)
    quarter = HEAD_DIM // 4
    partner = jnp.where((lane & quarter) == 0, pltpu.roll(x, HEAD_DIM - quarter, 1), pltpu.roll(x, quarter, 1))
    return x * cos + partner * sin


def _attn_kernel(sink_ref, q_ref, kp_ref, kc_ref, kn_ref, vp_ref, vc_ref, vn_ref, kx_ref, vx_ref,
                 cp_ref, cc_ref, cn_ref, sp_ref, sc_ref, sn_ref, o_ref, *, nct, nt):
    i = pl.program_id(1)
    nt_dims = (((1,), (1,)), ((), ()))
    rows = ATTN_GROUP * TT
    neg = jnp.float32(-jnp.inf)

    def scores(q, k):
        return lax.dot_general(q, k.astype(BF16), nt_dims, preferred_element_type=F32) * ATTN_SCALE

    def pv(p, v):
        return jnp.dot(p.astype(BF16), v.astype(BF16), preferred_element_type=F32)

    def head_slice(ref, g):
        return ref[0, :, g * HEAD_DIM:(g + 1) * HEAD_DIM]

    def sink_col(g):
        return jnp.concatenate(
            [jnp.full((TT, 1), sink_ref[g * ATTN_GROUP + j], F32) for j in range(ATTN_GROUP)], axis=0)

    def q_rows(g, rope):
        parts = [q_ref[0, :, (g * ATTN_GROUP + j) * HEAD_DIM:(g * ATTN_GROUP + j + 1) * HEAD_DIM]
                 for j in range(ATTN_GROUP)]
        return jnp.concatenate([rope(p) for p in parts], axis=0).astype(BF16)

    def store(g, o, den):
        o = o / den
        for j in range(ATTN_GROUP):
            c0 = (g * ATTN_GROUP + j) * HEAD_DIM
            o_ref[0, :, c0:c0 + HEAD_DIM] = o[j * TT:(j + 1) * TT].astype(o_ref.dtype)

    rmax = lambda s: jnp.max(s, axis=-1, keepdims=True)
    rsum = lambda p: jnp.sum(p, axis=-1, keepdims=True)

    @pl.when(i >= nct)
    def _():
        cos_c, sin_c = cc_ref[...], sc_ref[...]
        row = lax.broadcasted_iota(jnp.int32, (rows, TT), 0) & (TT - 1)
        col = lax.broadcasted_iota(jnp.int32, (rows, TT), 1)
        keep_prev = col >= row + jnp.where(i > nct, 0, TT)
        keep_next = col <= row - jnp.where(i < nt - 1, 0, TT)
        for g in range(ATTN_KV):
            q = q_rows(g, lambda x: _rope(x, cos_c, sin_c))
            s_p = jnp.where(keep_prev, scores(q, _rope(head_slice(kp_ref, g), cp_ref[...], sp_ref[...])), neg)
            s_c = scores(q, _rope(head_slice(kc_ref, g), cos_c, sin_c))
            s_n = jnp.where(keep_next, scores(q, _rope(head_slice(kn_ref, g), cn_ref[...], sn_ref[...])), neg)
            s_x = scores(q, head_slice(kx_ref, g))
            snk = sink_col(g)
            halves = lambda t: [t[:, c0:c0 + TT] for c0 in range(0, t.shape[1], TT)]
            m_el = functools.reduce(jnp.maximum, [s_p, s_c, s_n] + halves(s_x))
            m = jnp.maximum(rmax(m_el), snk)
            p_p, p_c, p_n, p_x = jnp.exp(s_p - m), jnp.exp(s_c - m), jnp.exp(s_n - m), jnp.exp(s_x - m)
            den = rsum(functools.reduce(jnp.add, [p_p, p_c, p_n] + halves(p_x))) + jnp.exp(snk - m)
            o = (pv(p_p, head_slice(vp_ref, g)) + pv(p_c, head_slice(vc_ref, g)) + pv(p_n, head_slice(vn_ref, g))
                 + pv(p_x, head_slice(vx_ref, g)))
            store(g, o, den)

    @pl.when(i < nct)
    def _():
        for g in range(ATTN_KV):
            q = q_rows(g, lambda x: x)
            s_x = scores(q, head_slice(kx_ref, g))
            snk = sink_col(g)
            halves = lambda t: [t[:, c0:c0 + TT] for c0 in range(0, t.shape[1], TT)]
            m = jnp.maximum(rmax(functools.reduce(jnp.maximum, halves(s_x))), snk)
            p_x = jnp.exp(s_x - m)
            store(g, pv(p_x, head_slice(vx_ref, g)), rsum(functools.reduce(jnp.add, halves(p_x))) + jnp.exp(snk - m))


def _rope_tables(n_ctx, s):
    quarter = HEAD_DIM // 4
    inv_freq = ROPE_BASE ** (-jnp.arange(quarter, dtype=F32) / quarter)
    pos = jnp.arange(s)
    ang_r = (pos // GRID_W).astype(F32)[:, None] * inv_freq[None, :]
    ang_c = (pos % GRID_W).astype(F32)[:, None] * inv_freq[None, :]
    cos = jnp.concatenate([jnp.cos(ang_r), jnp.cos(ang_r), jnp.cos(ang_c), jnp.cos(ang_c)], axis=-1)
    sin = jnp.concatenate([-jnp.sin(ang_r), jnp.sin(ang_r), -jnp.sin(ang_c), jnp.sin(ang_c)], axis=-1)
    pad = jnp.zeros((n_ctx, HEAD_DIM), F32)
    return jnp.concatenate([pad, cos], axis=0), jnp.concatenate([pad, sin], axis=0)


def _attention(proj, sink, cos, sin, n_ctx, nct, nt):
    nb, tt, _ = proj.shape
    kvw = ATTN_KV * HEAD_DIM
    prev_i = lambda i: jnp.maximum(i - 1, 0)
    next_i = lambda i: jnp.minimum(i + 1, nt - 1)

    def kv(base, which):
        return pl.BlockSpec((1, TT, kvw), lambda b, i: (b, which(i), base // kvw))

    def tab(which):
        return pl.BlockSpec((TT, HEAD_DIM), lambda b, i: (which(i), 0))

    same = lambda i: i
    in_specs = [pl.BlockSpec(memory_space=pltpu.SMEM),
                pl.BlockSpec((1, TT, D), lambda b, i: (b, i, AB_Q // D)),
                kv(AB_K, prev_i), kv(AB_K, same), kv(AB_K, next_i),
                kv(AB_V, prev_i), kv(AB_V, same), kv(AB_V, next_i),
                pl.BlockSpec((1, n_ctx, kvw), lambda b, i: (b, 0, AB_K // kvw)),
                pl.BlockSpec((1, n_ctx, kvw), lambda b, i: (b, 0, AB_V // kvw)),
                tab(prev_i), tab(same), tab(next_i), tab(prev_i), tab(same), tab(next_i)]
    return pl.pallas_call(
        functools.partial(_attn_kernel, nct=nct, nt=nt),
        grid=(nb, nt), in_specs=in_specs,
        out_specs=pl.BlockSpec((1, TT, D), lambda b, i: (b, i, 0)),
        out_shape=jax.ShapeDtypeStruct((nb, tt, D), BF16),
        compiler_params=_cparams("parallel", "arbitrary"), name="attention",
    )(sink.astype(F32), proj, proj, proj, proj, proj, proj, proj, proj, proj, cos, cos, cos, sin, sin, sin)


def _conformer_kernel(a_ref, ap_ref, an_ref, b_ref, bp_ref, bn_ref, w_ref, cb_ref, ln_ref, o_ref, ext_ref, acc_ref,
                      *, nct, nt):
    i = pl.program_id(1)
    has_prev, has_next = _segment_edges(i, nct, nt)
    _fill_ext(ext_ref, a_ref[0] * _sigmoid(b_ref[0]), ap_ref[0] * _sigmoid(bp_ref[0]), an_ref[0] * _sigmoid(bn_ref[0]),
              has_prev, has_next)

    def cols(c, carry):
        c0 = pl.multiple_of(c * LANE, LANE)
        acc_ref[:, pl.ds(c0, LANE)] = _dwconv_cols_grouped(ext_ref, w_ref, cb_ref, CONF_KERNEL, CONF_PAD_L, c0, LANE)
        return carry

    lax.fori_loop(0, D // LANE, cols, 0)
    u = acc_ref[...]
    mu = jnp.mean(u, axis=-1, keepdims=True)
    uc = u - mu
    var = jnp.mean(uc * uc, axis=-1, keepdims=True)
    o_ref[0] = _silu(uc * lax.rsqrt(var + LN_EPS) * ln_ref[0:1, :] + ln_ref[1:2, :]).astype(o_ref.dtype)


def _conformer(proj, dw_w, dw_b, ln_g, ln_b, nct, nt):
    nb, tt, _ = proj.shape
    full = lambda shape: pl.BlockSpec(shape, lambda b, i: (0,) * len(shape))
    return pl.pallas_call(
        functools.partial(_conformer_kernel, nct=nct, nt=nt),
        grid=(nb, nt),
        in_specs=[*_tile_specs(D, CD_GLU_A // D, nt), *_tile_specs(D, CD_GLU_B // D, nt),
                  full((CONF_KERNEL, D)), full((1, D)), full((2, D))],
        out_specs=pl.BlockSpec((1, TT, D), lambda b, i: (b, i, 0)),
        out_shape=jax.ShapeDtypeStruct((nb, tt, D), BF16),
        scratch_shapes=[pltpu.VMEM((TT + 2 * HALO, D), F32), pltpu.VMEM((TT, D), F32)],
        compiler_params=_cparams("parallel", "arbitrary"), name="conformer",
    )(proj, proj, proj, proj, proj, proj, dw_w, dw_b[None, :], jnp.stack([ln_g, ln_b]))


def _rg_kernel(*refs, direction, final, nct, nt):
    if final:
        (x_ref, xp_ref, xn_ref, w_ref, cb_ref, gw_ref, gb_ref, lam_ref, hf_ref, gate_ref, o_ref,
         ext_ref, a_ref, b_ref, h_ref, hs_ref) = refs
    else:
        x_ref, xp_ref, xn_ref, w_ref, cb_ref, gw_ref, gb_ref, lam_ref, o_ref, ext_ref, a_ref, b_ref, h_ref, hs_ref = refs
    reverse = direction == 1
    i = pl.program_id(1)

    @pl.when(i == 0)
    def _():
        h_ref[...] = jnp.zeros_like(h_ref)

    has_prev, has_next = _segment_edges(_scan_tile_index(nct, nt, reverse)(i), nct, nt)
    _fill_ext(ext_ref, x_ref[0], xp_ref[0], xn_ref[0], has_prev, has_next)
    for blk in range(RG_BLOCKS):
        c0 = blk * RG_BLOCK
        xr = _dwconv_cols(ext_ref, w_ref, cb_ref, SHORT_CONV, SHORT_PAD_L, c0, RG_BLOCK)
        pre = jnp.dot(xr.astype(BF16), gw_ref[blk], preferred_element_type=F32) + gb_ref[blk:blk + 1, :]
        r = _sigmoid(pre[:, :RG_BLOCK])
        gate_i = _sigmoid(pre[:, RG_BLOCK:])
        log_a = -RG_C * r * _softplus(-lam_ref[:, c0:c0 + RG_BLOCK])
        a_ref[:, c0:c0 + RG_BLOCK] = jnp.exp(log_a)
        b_ref[:, c0:c0 + RG_BLOCK] = jnp.sqrt(-_expm1(2.0 * log_a)) * gate_i * xr

    def step(s, h):
        t = (TT - 1 - s) if reverse else s
        h = a_ref[pl.ds(t, 1), :] * h + b_ref[pl.ds(t, 1), :]
        hs_ref[pl.ds(t, 1), :] = h
        return h

    h_ref[...] = lax.fori_loop(0, TT, step, h_ref[...])
    if final:
        o_ref[0] = ((hf_ref[0] + hs_ref[...]) * _gelu_tanh(gate_ref[0])).astype(o_ref.dtype)
    else:
        o_ref[0] = hs_ref[...]


def _rg_direction(proj, conv_w, conv_b, gate_w, gate_b, lam, direction, nct, nt, *, h_fwd=None):
    nb, tt, _ = proj.shape
    final = h_fwd is not None
    tile_of = _scan_tile_index(nct, nt, direction == 1)
    full = lambda shape: pl.BlockSpec(shape, lambda b, i: (0,) * len(shape))
    tok = lambda cb=0: pl.BlockSpec((1, TT, D), lambda bb, i: (bb, tile_of(i), cb))
    gw = jnp.transpose(gate_w[direction], (1, 2, 0, 3)).reshape(RG_BLOCKS, RG_BLOCK, 2 * RG_BLOCK).astype(BF16)
    gb = jnp.transpose(gate_b[direction].reshape(2, RG_BLOCKS, RG_BLOCK), (1, 0, 2)).reshape(RG_BLOCKS, 2 * RG_BLOCK)
    args = [proj, proj, proj, conv_w, conv_b[None, :], gw, gb.astype(F32), lam[direction][None, :].astype(F32)]
    in_specs = [*_tile_specs(D, CD_RX // D, nt, tile_of), full((SHORT_CONV, D)), full((1, D)),
                full((RG_BLOCKS, RG_BLOCK, 2 * RG_BLOCK)), full((RG_BLOCKS, 2 * RG_BLOCK)), full((1, D))]
    if final:
        args += [h_fwd, proj]
        in_specs += [tok(), tok(CD_RGATE // D)]
    return pl.pallas_call(
        functools.partial(_rg_kernel, direction=direction, final=final, nct=nct, nt=nt),
        grid=(nb, nt), in_specs=in_specs, out_specs=tok(),
        out_shape=jax.ShapeDtypeStruct((nb, tt, D), BF16 if final else F32),
        scratch_shapes=[pltpu.VMEM((TT + 2 * HALO, D), F32), pltpu.VMEM((TT, D), F32), pltpu.VMEM((TT, D), F32),
                        pltpu.VMEM((1, D), F32), pltpu.VMEM((TT, D), F32)],
        compiler_params=_cparams("parallel", "arbitrary"), name=f"rg_dir{direction}")(*args)


def _even_mixer(h, w_in, w_out, conv_w, conv_b, dt_bias, a_log, d_skip, norm_w, sink, cos, sin, n_ctx, nct, nt):
    nb, tt, _ = h.shape
    w = w_in
    w_perm = jnp.concatenate([w[:, 2048:4096], w[:, 0:2048], w[:, 5184:7232], w[:, 4096:5120], w[:, 7232:7744],
                              w[:, 7744:8256], w[:, 5120:5184], jnp.zeros((D, AB_N - 8256), w.dtype)], axis=1)
    proj = _matmul([h.reshape(nb * tt, D)], [w_perm.astype(BF16)], F32, "ab_in").reshape(nb, tt, AB_N)
    xs_c, bc_c, dt_sp = _ssd_prep(proj, conv_w, conv_b, dt_bias, nct, nt)
    y_fwd = _ssd_scan(xs_c, bc_c, dt_sp, a_log, 0, nct, nt, d_skip=d_skip)
    ssd_o = _ssd_scan(xs_c, bc_c, dt_sp, a_log, 1, nct, nt, y_in=y_fwd, proj=proj, norm_w=norm_w)
    att_o = _attention(proj, sink, cos, sin, n_ctx, nct, nt)
    wo = w_out.astype(BF16)
    return _matmul([ssd_o.reshape(nb * tt, D), att_o.reshape(nb * tt, D)], [wo[:D], wo[D:]], BF16, "ab_out")


def _odd_mixer(h, w_in, w_out, dw_w, dw_b, cln_g, cln_b, rconv_w, rconv_b, gate_w, gate_b, lam, nct, nt):
    nb, tt, _ = h.shape
    proj = _matmul([h.reshape(nb * tt, D)], [w_in.astype(BF16)], F32, "cd_in").reshape(nb, tt, CD_N)
    conf_o = _conformer(proj, dw_w, dw_b, cln_g, cln_b, nct, nt)
    h_fwd = _rg_direction(proj, rconv_w, rconv_b, gate_w, gate_b, lam, 0, nct, nt)
    rg_o = _rg_direction(proj, rconv_w, rconv_b, gate_w, gate_b, lam, 1, nct, nt, h_fwd=h_fwd)
    wo = w_out.astype(BF16)
    return _matmul([conf_o.reshape(nb * tt, D), rg_o.reshape(nb * tt, D)], [wo[:D], wo[D:]], BF16, "cd_out")


def kernel(x, c, ctx, c_ctx, mod_w, mod_b, ln_g, ln_b, ab_w_in, ab_w_out, ssd_conv_w, ssd_conv_b, ssd_dt_bias, ssd_a_log, ssd_d, ssd_norm_w, attn_sink, ffn_w1, ffn_w3, ffn_w2, cd_w_in, cd_w_out, conf_dw_w, conf_dw_b, conf_ln_g, conf_ln_b, rg_conv_w, rg_conv_b, rg_gate_w, rg_gate_b, rg_lambda, moe_router, moe_w1, moe_w3, moe_w2):
    nb, s, _ = x.shape
    n_ctx = ctx.shape[1]
    depth = mod_w.shape[0]
    assert s % TT == 0 and n_ctx % TT == 0 and n_ctx >= TT and x.shape[2] == D
    tt = n_ctx + s
    nct, nt = n_ctx // TT, tt // TT
    alpha = (2 * depth) ** 0.25

    modt = _modulation_tables(c, c_ctx, mod_w, mod_b)
    cos, sin = _rope_tables(n_ctx, s)
    xs = jnp.concatenate([ctx, x], axis=1)
    (h,) = _post(xs, n_ctx, nxt=(modt[0], 0, 1))
    for l in range(depth):
        j = l // 2
        if l % 2 == 0:
            mix = _even_mixer(h, ab_w_in[j], ab_w_out[j], ssd_conv_w[j], ssd_conv_b[j], ssd_dt_bias[j], ssd_a_log[j],
                              ssd_d[j], ssd_norm_w[j], attn_sink[j], cos, sin, n_ctx, nct, nt)
        else:
            mix = _odd_mixer(h, cd_w_in[j], cd_w_out[j], conf_dw_w[j], conf_dw_b[j], conf_ln_g[j], conf_ln_b[j],
                             rg_conv_w[j], rg_conv_b[j], rg_gate_w[j], rg_gate_b[j], rg_lambda[j], nct, nt)
        ln0 = jnp.stack([ln_g[l, 0], ln_b[l, 0]])
        ln1 = jnp.stack([ln_g[l, 1], ln_b[l, 1]])
        res = (mix.reshape(nb, tt, D), modt[l], 2, ln0, alpha)
        hf = h.reshape(nb * tt, D)
        if l % 2 == 0:
            xs, h = _post(xs, n_ctx, res=res, nxt=(modt[l], 3, 4))
            hf = h.reshape(nb * tt, D)
            act = _swiglu_up(hf, ffn_w1[j].astype(BF16), ffn_w3[j].astype(BF16), "ffn_up")
            f = _matmul([act], [ffn_w2[j].astype(BF16)], BF16, "ffn_down")
        else:
            wr = jnp.zeros((D, LANE), F32).at[:, :N_EXPERTS].set(moe_router[j])
            xs, h, comb = _post(xs, n_ctx, res=res, nxt=(modt[l], 3, 4), router=wr)
            hf = h.reshape(nb * tt, D)
            f = _moe_sparse(hf, comb.reshape(nb * tt, LANE), moe_w1[j], moe_w3[j], moe_w2[j])
        res = (f.reshape(nb, tt, D), modt[l], 5, ln1, alpha)
        if l + 1 < depth:
            xs, h = _post(xs, n_ctx, res=res, nxt=(modt[l + 1], 0, 1))
        else:
            (out,) = _post(xs, n_ctx, res=res, latent_only=True)
    return out
```

```python
import functools
import math

import jax
import jax.numpy as jnp
import numpy as np
from jax import lax
from jax.experimental import pallas as pl
from jax.experimental.pallas import tpu as pltpu

F32 = jnp.float32
BF16 = jnp.bfloat16
HIGHEST = lax.Precision.HIGHEST

LANE = 128
SUBLANE = 8
VMEM_BYTES = 64 * 1024 * 1024
VMEM_LIMIT = VMEM_BYTES * 7 // 8
MATMUL_VMEM_BUDGET = VMEM_BYTES * 5 // 8

D = 2048
GRID_W = 64
SSD_HEAD_DIM = 64
SSD_HEADS = D // SSD_HEAD_DIM
SSD_GROUPS = 4
SSD_STATE = 128
SSD_GROUP_HEADS = SSD_HEADS // SSD_GROUPS
SHORT_CONV = 4
SHORT_PAD_L = 2
HEAD_DIM = 128
ATTN_HEADS = D // HEAD_DIM
ATTN_KV = ATTN_HEADS // 4
ATTN_GROUP = ATTN_HEADS // ATTN_KV
WINDOW = 128
ATTN_SCALE = HEAD_DIM ** -0.5
ROPE_BASE = 10000.0
CONF_KERNEL = 31
CONF_PAD_L = 15
RG_BLOCK = 128
RG_BLOCKS = D // RG_BLOCK
RG_C = 8.0
N_EXPERTS = 8
ROUTE_I1, ROUTE_I2 = N_EXPERTS, N_EXPERTS + 1
LN_EPS = 1e-5

TT = 128
HALO = 16

AB_XS, AB_Z, AB_Q, AB_BC, AB_K, AB_V, AB_DT = 0, 2048, 4096, 6144, 7168, 7680, 8192
AB_N = 8448
CD_GLU_A, CD_GLU_B, CD_RX, CD_RGATE = 0, 2048, 4096, 6144
CD_N = 8192


def _cparams(*sem):
    return pltpu.CompilerParams(dimension_semantics=sem, vmem_limit_bytes=VMEM_LIMIT)


def _sigmoid(x):
    return 0.5 * (1.0 + jnp.tanh(0.5 * x))


def _silu(x):
    return x * _sigmoid(x)


def _softplus(x):
    return jnp.maximum(x, 0.0) + jnp.log1p(jnp.exp(-jnp.abs(x)))


def _gelu_tanh(x):
    return 0.5 * x * (1.0 + jnp.tanh(math.sqrt(2.0 / math.pi) * (x + 0.044715 * (x * x * x))))


def _mod_kernel(c_ref, w_ref, b_ref, o_ref):
    s = _silu(c_ref[...])
    o_ref[0] = jnp.dot(s, w_ref[0], preferred_element_type=F32, precision=HIGHEST) + b_ref[0]


def _modulation_tables(c, c_ctx, mod_w, mod_b):
    depth, _, n6 = mod_w.shape
    nb = c.shape[0]
    rows = -(-(nb + 1) // SUBLANE) * SUBLANE
    cc = jnp.zeros((rows, D), F32).at[:nb].set(c).at[nb].set(c_ctx)
    tn = 1024
    out = pl.pallas_call(
        _mod_kernel,
        grid=(depth, n6 // tn),
        in_specs=[pl.BlockSpec((rows, D), lambda l, j: (0, 0)),
                  pl.BlockSpec((1, D, tn), lambda l, j: (l, 0, j)),
                  pl.BlockSpec((1, 1, tn), lambda l, j: (l, 0, j))],
        out_specs=pl.BlockSpec((1, rows, tn), lambda l, j: (l, 0, j)),
        out_shape=jax.ShapeDtypeStruct((depth, rows, n6), F32),
        compiler_params=_cparams("arbitrary", "arbitrary"),
        name="modulation",
    )(cc, mod_w, mod_b.reshape(depth, 1, n6))
    lat = out[:, :nb].reshape(depth, nb, 1, 6, D)
    ctx = jnp.broadcast_to(out[:, nb].reshape(depth, 1, 1, 6, D), (depth, nb, 1, 6, D))
    return jnp.concatenate([ctx, lat], axis=2)


def _post_kernel(*refs, alpha, gate_row, mod_rows, has_res, has_h, has_router):
    it = iter(refs)
    x_ref = next(it)
    if has_res:
        m_ref, modg_ref, ln_ref = next(it), next(it), next(it)
    if has_h:
        modn_ref = next(it)
    if has_router:
        wr_ref = next(it)
    if has_res:
        xo_ref = next(it)
    if has_h:
        h_ref = next(it)
    if has_router:
        comb_ref = next(it)

    x = x_ref[0]
    if has_res:
        g = modg_ref[0, 0, gate_row:gate_row + 1, :]
        xf = alpha * x + g * m_ref[0].astype(F32)
        mu = jnp.mean(xf, axis=-1, keepdims=True)
        xc = xf - mu
        var = jnp.mean(xc * xc, axis=-1, keepdims=True)
        x = xc * lax.rsqrt(var + LN_EPS) * ln_ref[0:1, :] + ln_ref[1:2, :]
        xo_ref[0] = x
    if has_h:
        sh = modn_ref[0, 0, mod_rows[0]:mod_rows[0] + 1, :]
        sc = modn_ref[0, 0, mod_rows[1]:mod_rows[1] + 1, :]
        h = x * (1.0 + sc) + sh
        h_ref[0] = h.astype(BF16)
        if has_router:
            logits = jnp.dot(h, wr_ref[...], preferred_element_type=F32, precision=HIGHEST)
            lane = lax.broadcasted_iota(jnp.int32, logits.shape, 1).astype(F32)
            neg = jnp.float32(-jnp.inf)
            lg = jnp.where(lane < N_EXPERTS, logits, neg)
            m1 = jnp.max(lg, axis=-1, keepdims=True)
            i1 = jnp.min(jnp.where(lg == m1, lane, float(LANE)), axis=-1, keepdims=True)
            lg2 = jnp.where(lane == i1, neg, lg)
            m2 = jnp.max(lg2, axis=-1, keepdims=True)
            i2 = jnp.min(jnp.where(lg2 == m2, lane, float(LANE)), axis=-1, keepdims=True)
            e2 = jnp.exp(m2 - m1)
            den = 1.0 + e2
            gates = jnp.where(lane == i1, 1.0 / den, 0.0) + jnp.where(lane == i2, e2 / den, 0.0)
            comb_ref[0] = jnp.where(lane == ROUTE_I1, i1, jnp.where(lane == ROUTE_I2, i2, gates))


def _post(x, n_ctx, *, res=None, nxt=None, router=None, latent_only=False):
    nb, tt, _ = x.shape
    tm = 256 if n_ctx % 256 == 0 else TT
    nct = n_ctx // tm
    first = nct if latent_only else 0
    tt_out = tt - first * tm
    tok_in = pl.BlockSpec((1, tm, D), lambda b, i: (b, i + first, 0))
    tok = pl.BlockSpec((1, tm, D), lambda b, i: (b, i, 0))
    modspec = pl.BlockSpec((1, 1, 6, D), lambda b, i: (b, jnp.where(i + first >= nct, 1, 0), 0, 0))
    args, in_specs, out_shape, out_specs = [x], [tok_in], [], []
    if res is not None:
        m, modt, gate_row, ln_gb, alpha = res
        args += [m, modt, ln_gb]
        in_specs += [tok_in, modspec, pl.BlockSpec((2, D), lambda b, i: (0, 0))]
        out_shape.append(jax.ShapeDtypeStruct((nb, tt_out, D), F32))
        out_specs.append(tok)
    else:
        gate_row, alpha = 0, 1.0
    if nxt is not None:
        args.append(nxt[0])
        in_specs.append(modspec)
        out_shape.append(jax.ShapeDtypeStruct((nb, tt_out, D), BF16))
        out_specs.append(tok)
    if router is not None:
        args.append(router)
        in_specs.append(pl.BlockSpec((D, LANE), lambda b, i: (0, 0)))
        out_shape.append(jax.ShapeDtypeStruct((nb, tt_out, LANE), F32))
        out_specs.append(pl.BlockSpec((1, tm, LANE), lambda b, i: (b, i, 0)))
    kern = functools.partial(_post_kernel, alpha=alpha, gate_row=gate_row,
                             mod_rows=(nxt[1], nxt[2]) if nxt is not None else (0, 0),
                             has_res=res is not None, has_h=nxt is not None, has_router=router is not None)
    return pl.pallas_call(kern, grid=(nb, tt_out // tm), in_specs=in_specs, out_specs=out_specs, out_shape=out_shape,
                          compiler_params=_cparams("parallel", "arbitrary"), name="post")(*args)


def _mm_kernel(*refs, n_a):
    o_ref = refs[-1]
    acc = jnp.dot(refs[0][...], refs[n_a][...], preferred_element_type=F32)
    for t in range(1, n_a):
        acc = acc + jnp.dot(refs[t][...], refs[n_a + t][...], preferred_element_type=F32)
    o_ref[...] = acc.astype(o_ref.dtype)


def _glu_kernel(a_ref, w1_ref, w3_ref, o_ref):
    a = a_ref[...]
    g = jnp.dot(a, w1_ref[...], preferred_element_type=F32)
    u = jnp.dot(a, w3_ref[...], preferred_element_type=F32)
    o_ref[...] = (_silu(g) * u).astype(o_ref.dtype)


def _pick_tiles(m, n, k_total, n_w, out_bytes):
    best = None
    for tm in (1024, 512, 256, 128):
        if m % tm:
            continue
        for tn in (1024, 768, 512, 256, 128):
            if n % tn:
                continue
            need = 2 * (tm * k_total * 2 + k_total * tn * 2 * n_w + tm * tn * out_bytes)
            if need <= MATMUL_VMEM_BUDGET and (best is None or tm * tn > best[0] * best[1]):
                best = (tm, tn)
    assert best is not None, (m, n, k_total)
    return best


def _matmul(a_list, w_list, out_dtype, name):
    m, n = a_list[0].shape[0], w_list[0].shape[1]
    k_total = sum(a.shape[1] for a in a_list)
    tm, tn = _pick_tiles(m, n, k_total, 1, jnp.dtype(out_dtype).itemsize)
    in_specs = ([pl.BlockSpec((tm, a.shape[1]), lambda j, i: (i, 0)) for a in a_list]
                + [pl.BlockSpec((w.shape[0], tn), lambda j, i: (0, j)) for w in w_list])
    return pl.pallas_call(
        functools.partial(_mm_kernel, n_a=len(a_list)),
        grid=(n // tn, m // tm), in_specs=in_specs,
        out_specs=pl.BlockSpec((tm, tn), lambda j, i: (i, j)),
        out_shape=jax.ShapeDtypeStruct((m, n), out_dtype),
        compiler_params=_cparams("parallel", "arbitrary"), name=name)(*a_list, *w_list)


def _swiglu_up(a, w1, w3, name):
    m, k = a.shape
    n = w1.shape[1]
    tm, tn = _pick_tiles(m, n, k, 2, 2)
    return pl.pallas_call(
        _glu_kernel, grid=(n // tn, m // tm),
        in_specs=[pl.BlockSpec((tm, k), lambda j, i: (i, 0)),
                  pl.BlockSpec((k, tn), lambda j, i: (0, j)),
                  pl.BlockSpec((k, tn), lambda j, i: (0, j))],
        out_specs=pl.BlockSpec((tm, tn), lambda j, i: (i, j)),
        out_shape=jax.ShapeDtypeStruct((m, n), BF16),
        compiler_params=_cparams("parallel", "arbitrary"), name=name)(a, w1, w3)


MOE_SUB = 512
MOE_ROWS = 512
MOE_ALIGN = 16
MOE_BLK = 64
MOE_CAT = -(-(2 * MOE_SUB + N_EXPERTS * (MOE_ALIGN - 1 + MOE_BLK - 1)) // LANE) * LANE


def _moe_rank_kernel(route_ref, tri_ref, rank_ref, cnt_ref):
    route = route_ref[...]
    lane = lax.broadcasted_iota(jnp.int32, route.shape, 1).astype(F32)
    onehot = jnp.where(lane == route[:, ROUTE_I1:ROUTE_I1 + 1], 1.0,
                       jnp.where(lane == route[:, ROUTE_I2:ROUTE_I2 + 1], 1.0, 0.0))
    rank_ref[...] = jnp.dot(tri_ref[...], onehot.astype(BF16), preferred_element_type=F32)
    cnt_ref[0] = jnp.broadcast_to(jnp.sum(onehot, axis=0, keepdims=True), (SUBLANE, LANE))


def _moe_plan(cnt, r_tiles):
    c16 = -(-cnt // MOE_ALIGN) * MOE_ALIGN
    nblk = -(-c16 // MOE_BLK)
    locw = nblk * MOE_BLK
    loc = jnp.cumsum(locw, axis=1) - locw
    tot = jnp.sum(c16, axis=0)
    region = -(-(tot + MOE_BLK) // MOE_ROWS) * MOE_ROWS
    base = jnp.cumsum(region) - region
    off = base[None, :] + jnp.cumsum(c16, axis=0) - c16
    ends = jnp.cumsum(region // MOE_ROWS)
    tile_expert = jnp.minimum(jnp.searchsorted(ends, jnp.arange(r_tiles, dtype=jnp.int32), side="right"),
                              N_EXPERTS - 1).astype(jnp.int32)
    flat = lambda a: a.reshape(-1).astype(jnp.int32)
    return flat(off), flat(nblk), flat(loc), tile_expert, ends[-1:].astype(jnp.int32)


def _moe_positions(route, rank, loc_ref, s):
    lane = lax.broadcasted_iota(jnp.int32, route.shape, 1)
    locv = jnp.zeros(route.shape, F32)
    for e in range(N_EXPERTS):
        locv = jnp.where(lane == e, loc_ref[s * N_EXPERTS + e].astype(F32), locv)
    pos = rank + locv
    lanef = lane.astype(F32)
    sel1 = lanef == route[:, ROUTE_I1:ROUTE_I1 + 1]
    sel2 = lanef == route[:, ROUTE_I2:ROUTE_I2 + 1]
    pick = lambda sel, v: jnp.sum(jnp.where(sel, v, 0.0), axis=-1, keepdims=True)
    return pick(sel1, pos), pick(sel2, pos), pick(sel1, route), pick(sel2, route)


def _moe_segment_loops(nblk_ref, s, make_copies):
    for phase in ("start", "wait"):
        for e in range(N_EXPERTS):
            def body(rb, carry, e=e, phase=phase):
                for cp in make_copies(e, rb):
                    getattr(cp, phase)()
                return carry

            lax.fori_loop(0, nblk_ref[s * N_EXPERTS + e], body, 0)


def _moe_compact_kernel(off_ref, nblk_ref, loc_ref, h_ref, route_ref, rank_ref, xz_ref, gz_ref, xs_ref, gs_ref,
                        p_ref, xcat_ref, gcat_ref, sem):
    del xz_ref, gz_ref
    s = pl.program_id(0)
    lp0, lp1, g0, g1 = _moe_positions(route_ref[...], rank_ref[...], loc_ref, s)
    lane = lax.broadcasted_iota(jnp.int32, (MOE_SUB, LANE), 1)
    packed = jnp.where(lane == 0, lp0, jnp.where(lane == 1, lp1, jnp.where(lane == 2, g0, jnp.where(lane == 3, g1, 0.0))))
    pt = packed.T
    lp0r, lp1r, g0r, g1r = pt[0:1], pt[1:2], pt[2:3], pt[3:4]
    for c in range(MOE_CAT // LANE):
        r = (lax.broadcasted_iota(jnp.int32, (LANE, MOE_SUB), 0) + c * LANE).astype(F32)
        m0 = r == lp0r
        m1 = r == lp1r
        p_ref[c * LANE:(c + 1) * LANE, :] = jnp.where(m0, 1.0, jnp.where(m1, 1.0, 0.0)).astype(BF16)
        gate = jnp.sum(jnp.where(m0, g0r, 0.0) + jnp.where(m1, g1r, 0.0), axis=-1, keepdims=True)
        gcat_ref[c * LANE:(c + 1) * LANE, :] = jnp.broadcast_to(gate, (LANE, LANE))
    xcat_ref[...] = jnp.dot(p_ref[...], h_ref[...], preferred_element_type=F32).astype(BF16)

    def copies(e, rb):
        src = pl.multiple_of(loc_ref[s * N_EXPERTS + e] + rb * MOE_BLK, MOE_BLK)
        dst = pl.multiple_of(off_ref[s * N_EXPERTS + e] + rb * MOE_BLK, MOE_ALIGN)
        return (pltpu.make_async_copy(xcat_ref.at[pl.ds(src, MOE_BLK)], xs_ref.at[pl.ds(dst, MOE_BLK)], sem.at[0]),
                pltpu.make_async_copy(gcat_ref.at[pl.ds(src, MOE_BLK)], gs_ref.at[pl.ds(dst, MOE_BLK)], sem.at[1]))

    _moe_segment_loops(nblk_ref, s, copies)


def _moe_uncompact_kernel(off_ref, nblk_ref, loc_ref, route_ref, rank_ref, ys_ref, o_ref, pt_ref, ycat_ref, sem):
    s = pl.program_id(0)

    @pl.when(s == 0)
    def _():
        ycat_ref[...] = jnp.zeros_like(ycat_ref)

    def copies(e, rb):
        src = pl.multiple_of(off_ref[s * N_EXPERTS + e] + rb * MOE_BLK, MOE_ALIGN)
        dst = pl.multiple_of(loc_ref[s * N_EXPERTS + e] + rb * MOE_BLK, MOE_BLK)
        return (pltpu.make_async_copy(ys_ref.at[pl.ds(src, MOE_BLK)], ycat_ref.at[pl.ds(dst, MOE_BLK)], sem.at[0]),)

    _moe_segment_loops(nblk_ref, s, copies)
    lp0, lp1, _, _ = _moe_positions(route_ref[...], rank_ref[...], loc_ref, s)
    for c in range(MOE_CAT // LANE):
        col = (lax.broadcasted_iota(jnp.int32, (MOE_SUB, LANE), 1) + c * LANE).astype(F32)
        pt_ref[:, c * LANE:(c + 1) * LANE] = jnp.where(col == lp0, 1.0, jnp.where(col == lp1, 1.0, 0.0)).astype(BF16)
    o_ref[...] = jnp.dot(pt_ref[...], ycat_ref[...], preferred_element_type=F32).astype(o_ref.dtype)


def _expert_block_is_new(te_ref, i):
    return jnp.logical_or(i == 0, te_ref[i] != te_ref[jnp.maximum(i - 1, 0)])


def _moe_up_kernel(te_ref, nu_ref, x_ref, w1_ref, w3_ref, o_ref, w1b_ref, w3b_ref):
    i = pl.program_id(1)

    @pl.when(_expert_block_is_new(te_ref, i))
    def _():
        w1b_ref[...] = w1_ref[0, 0].astype(BF16)
        w3b_ref[...] = w3_ref[0, 0].astype(BF16)

    @pl.when(i < nu_ref[0])
    def _():
        a = x_ref[...]
        g = jnp.dot(a, w1b_ref[...], preferred_element_type=F32)
        u = jnp.dot(a, w3b_ref[...], preferred_element_type=F32)
        o_ref[...] = (_silu(g) * u).astype(o_ref.dtype)

    @pl.when(i >= nu_ref[0])
    def _():
        o_ref[...] = jnp.zeros_like(o_ref)


def _moe_down_kernel(te_ref, nu_ref, a_ref, w2_ref, g_ref, o_ref, w2b_ref):
    i = pl.program_id(1)

    @pl.when(_expert_block_is_new(te_ref, i))
    def _():
        w2b_ref[...] = w2_ref[0, 0].astype(BF16)

    @pl.when(i < nu_ref[0])
    def _():
        y = jnp.dot(a_ref[...], w2b_ref[...], preferred_element_type=F32)
        o_ref[...] = (g_ref[:, 0:1] * y).astype(o_ref.dtype)

    @pl.when(i >= nu_ref[0])
    def _():
        o_ref[...] = jnp.zeros_like(o_ref)


def _moe_sparse(h, route, w1, w3, w2, layer):
    t, k = h.shape
    _, ne, _, f = w1.shape
    assert t % MOE_SUB == 0 and ne == N_EXPERTS
    n_sub = t // MOE_SUB
    r_tiles = -(-(2 * t + n_sub * ne * (MOE_ALIGN - 1) + ne * (MOE_BLK + MOE_ROWS - 1)) // MOE_ROWS)
    r_rows = r_tiles * MOE_ROWS
    sub = lambda w: pl.BlockSpec((MOE_SUB, w), lambda s, *_: (s, 0))
    any_spec = pl.BlockSpec(memory_space=pl.ANY)

    tri = jnp.asarray(np.tril(np.ones((MOE_SUB, MOE_SUB), np.float32), -1), BF16)
    rank, cnt = pl.pallas_call(
        _moe_rank_kernel, grid=(n_sub,),
        in_specs=[sub(LANE), pl.BlockSpec((MOE_SUB, MOE_SUB), lambda s: (0, 0))],
        out_specs=[sub(LANE), pl.BlockSpec((1, SUBLANE, LANE), lambda s: (s, 0, 0))],
        out_shape=[jax.ShapeDtypeStruct((t, LANE), F32), jax.ShapeDtypeStruct((n_sub, SUBLANE, LANE), F32)],
        compiler_params=_cparams("parallel"), name="moe_rank")(route, tri)
    off, nblk, loc, tile_expert, n_used = _moe_plan(cnt[:, 0, :ne].astype(jnp.int32), r_tiles)

    x_sorted, gate_sorted = pl.pallas_call(
        _moe_compact_kernel,
        grid_spec=pltpu.PrefetchScalarGridSpec(
            num_scalar_prefetch=3, grid=(n_sub,),
            in_specs=[sub(k), sub(LANE), sub(LANE), any_spec, any_spec],
            out_specs=[any_spec, any_spec],
            scratch_shapes=[pltpu.VMEM((MOE_CAT, MOE_SUB), BF16), pltpu.VMEM((MOE_CAT, k), BF16),
                            pltpu.VMEM((MOE_CAT, LANE), F32), pltpu.SemaphoreType.DMA((2,))]),
        out_shape=[jax.ShapeDtypeStruct((r_rows, k), BF16), jax.ShapeDtypeStruct((r_rows, LANE), F32)],
        input_output_aliases={6: 0, 7: 1},
        compiler_params=_cparams("arbitrary"), name="moe_compact",
    )(off, nblk, loc, h, route, rank, jnp.zeros((r_rows, k), BF16), jnp.zeros((r_rows, LANE), F32))

    row_tile = lambda j, i, te, nu: jnp.minimum(i, nu[0] - 1)
    tn = 1024
    act = pl.pallas_call(
        _moe_up_kernel,
        grid_spec=pltpu.PrefetchScalarGridSpec(
            num_scalar_prefetch=2, grid=(f // tn, r_tiles),
            in_specs=[pl.BlockSpec((MOE_ROWS, k), lambda j, i, te, nu: (row_tile(j, i, te, nu), 0)),
                      pl.BlockSpec((1, 1, k, tn), lambda j, i, te, nu: (layer, te[i], 0, j)),
                      pl.BlockSpec((1, 1, k, tn), lambda j, i, te, nu: (layer, te[i], 0, j))],
            out_specs=pl.BlockSpec((MOE_ROWS, tn), lambda j, i, te, nu: (i, j)),
            scratch_shapes=[pltpu.VMEM((k, tn), BF16), pltpu.VMEM((k, tn), BF16)]),
        out_shape=jax.ShapeDtypeStruct((r_rows, f), BF16),
        compiler_params=_cparams("parallel", "arbitrary"), name="moe_up")(tile_expert, n_used, x_sorted, w1, w3)
    y_sorted = pl.pallas_call(
        _moe_down_kernel,
        grid_spec=pltpu.PrefetchScalarGridSpec(
            num_scalar_prefetch=2, grid=(k // tn, r_tiles),
            in_specs=[pl.BlockSpec((MOE_ROWS, f), lambda j, i, te, nu: (row_tile(j, i, te, nu), 0)),
                      pl.BlockSpec((1, 1, f, tn), lambda j, i, te, nu: (layer, te[i], 0, j)),
                      pl.BlockSpec((MOE_ROWS, LANE), lambda j, i, te, nu: (row_tile(j, i, te, nu), 0))],
            out_specs=pl.BlockSpec((MOE_ROWS, tn), lambda j, i, te, nu: (i, j)),
            scratch_shapes=[pltpu.VMEM((f, tn), BF16)]),
        out_shape=jax.ShapeDtypeStruct((r_rows, k), BF16),
        compiler_params=_cparams("parallel", "arbitrary"), name="moe_down")(tile_expert, n_used, act, w2, gate_sorted)

    return pl.pallas_call(
        _moe_uncompact_kernel,
        grid_spec=pltpu.PrefetchScalarGridSpec(
            num_scalar_prefetch=3, grid=(n_sub,),
            in_specs=[sub(LANE), sub(LANE), any_spec],
            out_specs=sub(k),
            scratch_shapes=[pltpu.VMEM((MOE_SUB, MOE_CAT), BF16), pltpu.VMEM((MOE_CAT, k), BF16),
                            pltpu.SemaphoreType.DMA((1,))]),
        out_shape=jax.ShapeDtypeStruct((t, k), BF16),
        compiler_params=_cparams("arbitrary"), name="moe_uncompact")(off, nblk, loc, route, rank, y_sorted)


def _tile_specs(width, col_block, n_tiles, tile_of=lambda i: i):
    per = TT // HALO
    last = n_tiles * per - 1
    cur = pl.BlockSpec((1, TT, width), lambda b, i: (b, tile_of(i), col_block))
    prev = pl.BlockSpec((1, HALO, width), lambda b, i: (b, jnp.maximum(tile_of(i) * per - 1, 0), col_block))
    nxt = pl.BlockSpec((1, HALO, width), lambda b, i: (b, jnp.minimum((tile_of(i) + 1) * per, last), col_block))
    return cur, prev, nxt


def _segment_edges(i, nct, nt):
    has_prev = jnp.logical_and(i != 0, i != nct)
    has_next = jnp.logical_and(i != nct - 1, i != nt - 1)
    return has_prev, has_next


def _fill_ext(ext_ref, cur, prev, nxt, has_prev, has_next):
    ext_ref[0:HALO, :] = jnp.where(has_prev, prev, 0.0)
    ext_ref[HALO:HALO + TT, :] = cur
    ext_ref[HALO + TT:HALO + TT + HALO, :] = jnp.where(has_next, nxt, 0.0)


def _dwconv_cols(ext_ref, w_ref, b_ref, taps, pad_l, c0, width):
    acc = jnp.broadcast_to(b_ref[:, pl.ds(c0, width)], (TT, width))
    for k in range(taps):
        off = HALO + k - pad_l
        acc = acc + ext_ref[off:off + TT, pl.ds(c0, width)] * w_ref[k:k + 1, pl.ds(c0, width)]
    return acc


def _dwconv_cols_grouped(ext_ref, w_ref, b_ref, taps, pad_l, c0, width):
    acc = jnp.broadcast_to(b_ref[:, pl.ds(c0, width)], (TT, width))
    rows = TT + SUBLANE
    for r in range(SUBLANE):
        part = None
        for k in range(taps):
            off = HALO + k - pad_l
            if off % SUBLANE != r:
                continue
            term = ext_ref[off - r:off - r + rows, pl.ds(c0, width)] * w_ref[k:k + 1, pl.ds(c0, width)]
            part = term if part is None else part + term
        if part is not None:
            acc = acc + part[r:r + TT]
    return acc


def _scan_tile_index(nct, nt, reverse):
    if not reverse:
        return lambda i: i
    return lambda i: jnp.where(i < nct, nct - 1 - i, nt + nct - 1 - i)


CONV_COLS = 256


def _ssd_prep_kernel(xs_ref, xsp_ref, xsn_ref, bc_ref, bcp_ref, bcn_ref, dt_ref, wx_ref, bx_ref, wb_ref, bb_ref,
                     dtb_ref, xo_ref, bo_ref, dto_ref, extx_ref, extb_ref, *, nct, nt):
    i = pl.program_id(1)
    has_prev, has_next = _segment_edges(i, nct, nt)
    _fill_ext(extx_ref, xs_ref[0], xsp_ref[0], xsn_ref[0], has_prev, has_next)
    _fill_ext(extb_ref, bc_ref[0], bcp_ref[0], bcn_ref[0], has_prev, has_next)
    for c0 in range(0, D, CONV_COLS):
        xo_ref[0, :, c0:c0 + CONV_COLS] = _silu(_dwconv_cols(extx_ref, wx_ref, bx_ref, SHORT_CONV, SHORT_PAD_L, c0, CONV_COLS))
    for c0 in range(0, 2 * SSD_GROUPS * SSD_STATE, CONV_COLS):
        bo_ref[0, :, c0:c0 + CONV_COLS] = _silu(_dwconv_cols(extb_ref, wb_ref, bb_ref, SHORT_CONV, SHORT_PAD_L, c0, CONV_COLS))
    dto_ref[0] = _softplus(dt_ref[0] + dtb_ref[...])


def _ssd_prep(proj, conv_w, conv_b, dt_bias, nct, nt):
    nb, tt, _ = proj.shape
    nbc = 2 * SSD_GROUPS * SSD_STATE
    xs_specs = _tile_specs(D, AB_XS // D, nt)
    bc_specs = _tile_specs(nbc, AB_BC // nbc, nt)
    full = lambda shape: pl.BlockSpec(shape, lambda b, i: (0,) * len(shape))
    dtb = jnp.zeros((1, LANE), F32).at[0, :2 * SSD_HEADS].set(dt_bias.reshape(-1))
    return pl.pallas_call(
        functools.partial(_ssd_prep_kernel, nct=nct, nt=nt),
        grid=(nb, nt),
        in_specs=[*xs_specs, *bc_specs,
                  pl.BlockSpec((1, TT, LANE), lambda b, i: (b, i, AB_DT // LANE)),
                  full((SHORT_CONV, D)), full((1, D)), full((SHORT_CONV, nbc)), full((1, nbc)), full((1, LANE))],
        out_specs=[pl.BlockSpec((1, TT, D), lambda b, i: (b, i, 0)),
                   pl.BlockSpec((1, TT, nbc), lambda b, i: (b, i, 0)),
                   pl.BlockSpec((1, TT, LANE), lambda b, i: (b, i, 0))],
        out_shape=[jax.ShapeDtypeStruct((nb, tt, D), F32), jax.ShapeDtypeStruct((nb, tt, nbc), F32),
                   jax.ShapeDtypeStruct((nb, tt, LANE), F32)],
        scratch_shapes=[pltpu.VMEM((TT + 2 * HALO, D), F32), pltpu.VMEM((TT + 2 * HALO, nbc), F32)],
        compiler_params=_cparams("parallel", "arbitrary"), name="ssd_prep",
    )(proj, proj, proj, proj, proj, proj, proj, conv_w[:, :D], conv_b[None, :D], conv_w[:, D:], conv_b[None, D:], dtb)


def _ssd_scan_kernel(*refs, direction, final):
    if final:
        xs_ref, bc_ref, dt_ref, aneg_ref, tri_ref, yin_ref, z_ref, nw_ref, o_ref, h_ref, y_ref = refs
    else:
        xs_ref, bc_ref, dt_ref, aneg_ref, tri_ref, dsk_ref, o_ref, h_ref = refs
    forward = direction == 0

    @pl.when(pl.program_id(1) == 0)
    def _():
        h_ref[...] = jnp.zeros_like(h_ref)

    dt = dt_ref[0]
    d_a = dt * aneg_ref[...]
    a_cs = jnp.dot(tri_ref[...], d_a, preferred_element_type=F32, precision=HIGHEST)
    end = TT - 1 if forward else 0
    tot = a_cs[end:end + 1, :]
    w_end = dt * jnp.exp(tot - a_cs)
    e_in = jnp.exp(a_cs)
    e_tot = jnp.exp(tot)
    a_cs_t, w_end_t, dt_t = a_cs.T, w_end.T, dt.T
    row = lax.broadcasted_iota(jnp.int32, (TT, TT), 0)
    col = lax.broadcasted_iota(jnp.int32, (TT, TT), 1)
    causal = (row >= col) if forward else (row <= col)
    low = col < SSD_HEAD_DIM
    neg = jnp.float32(-jnp.inf)
    ns = SSD_GROUPS * SSD_STATE
    pair_w = 2 * SSD_HEAD_DIM

    for g in range(SSD_GROUPS):
        b_g = bc_ref[0, :, g * SSD_STATE:(g + 1) * SSD_STATE]
        c_g = bc_ref[0, :, ns + g * SSD_STATE:ns + (g + 1) * SSD_STATE].astype(BF16)
        scores = lax.dot_general(c_g, b_g.astype(BF16), (((1,), (1,)), ((), ())), preferred_element_type=F32)
        b_t = b_g.T
        gw = SSD_GROUP_HEADS * SSD_HEAD_DIM
        y_off = jnp.dot(c_g, h_ref[:, g * gw:(g + 1) * gw].astype(BF16), preferred_element_type=F32)
        for pr in range(SSD_GROUP_HEADS // 2):
            head = g * SSD_GROUP_HEADS + 2 * pr
            ca = direction * SSD_HEADS + head
            cb = ca + 1
            c0 = head * SSD_HEAD_DIM
            xs_pair = xs_ref[0, :, c0:c0 + pair_w]
            rhs = jnp.concatenate([jnp.where(low, xs_pair, 0.0), jnp.where(low, 0.0, xs_pair)], axis=0).astype(BF16)

            def head_lhs(c):
                seg = a_cs[:, c:c + 1] - a_cs_t[c:c + 1, :]
                m = scores * jnp.exp(jnp.where(causal, seg, neg)) * dt_t[c:c + 1, :]
                return m, b_t * w_end_t[c:c + 1, :]

            m_a, bw_a = head_lhs(ca)
            m_b, bw_b = head_lhs(cb)
            lhs = jnp.concatenate([jnp.concatenate([m_a, m_b], axis=1),
                                   jnp.concatenate([bw_a, bw_b], axis=1)], axis=0).astype(BF16)
            res = jnp.dot(lhs, rhs, preferred_element_type=F32)
            e_pair = jnp.where(low, e_in[:, ca:ca + 1], e_in[:, cb:cb + 1])
            y_pair = res[:TT] + y_off[:, 2 * pr * SSD_HEAD_DIM:2 * pr * SSD_HEAD_DIM + pair_w] * e_pair
            et_pair = jnp.where(low[0:1], e_tot[:, ca:ca + 1], e_tot[:, cb:cb + 1])
            h_ref[:, c0:c0 + pair_w] = h_ref[:, c0:c0 + pair_w] * et_pair + res[TT:]
            if final:
                y_ref[:, c0:c0 + pair_w] = yin_ref[0, :, c0:c0 + pair_w] + y_pair
            else:
                o_ref[0, :, c0:c0 + pair_w] = xs_pair * dsk_ref[:, c0:c0 + pair_w] + y_pair

    if final:
        u = y_ref[...] * _silu(z_ref[0])
        o_ref[0] = (u * lax.rsqrt(jnp.mean(u * u, axis=-1, keepdims=True) + LN_EPS) * nw_ref[...]).astype(o_ref.dtype)


def _ssd_scan(xs_c, bc_c, dt_sp, a_log, direction, nct, nt, *, d_skip=None, y_in=None, proj=None, norm_w=None):
    nb, tt, _ = xs_c.shape
    nbc = bc_c.shape[-1]
    final = y_in is not None
    tile_of = _scan_tile_index(nct, nt, direction == 1)
    tok = lambda w, cb=0: pl.BlockSpec((1, TT, w), lambda b, i: (b, tile_of(i), cb))
    full = lambda shape: pl.BlockSpec(shape, lambda b, i: (0,) * len(shape))
    aneg = jnp.zeros((1, LANE), F32).at[0, :2 * SSD_HEADS].set(-jnp.exp(a_log.astype(F32)).reshape(-1))
    ones = np.ones((TT, TT), np.float32)
    tri = jnp.asarray(np.tril(ones) if direction == 0 else np.triu(ones))
    args = [xs_c, bc_c, dt_sp, aneg, tri]
    in_specs = [tok(D), tok(nbc), tok(LANE), full((1, LANE)), full((TT, TT))]
    scratch = [pltpu.VMEM((SSD_STATE, D), F32)]
    if final:
        args += [y_in, proj, norm_w[None, :]]
        in_specs += [tok(D), tok(D, AB_Z // D), full((1, D))]
        scratch.append(pltpu.VMEM((TT, D), F32))
        out_dtype = BF16
    else:
        args.append(jnp.repeat(d_skip.astype(F32), SSD_HEAD_DIM)[None, :])
        in_specs.append(full((1, D)))
        out_dtype = F32
    return pl.pallas_call(
        functools.partial(_ssd_scan_kernel, direction=direction, final=final),
        grid=(nb, nt), in_specs=in_specs, out_specs=tok(D),
        out_shape=jax.ShapeDtypeStruct((nb, tt, D), out_dtype), scratch_shapes=scratch,
        compiler_params=_cparams("parallel", "arbitrary"), name=f"ssd_scan{direction}")(*args)


def _rope(x, cos, sin):
    lane = lax.broadcasted_iota(jnp.int32, x.shape, 1)
    quarter = HEAD_DIM // 4
    partner = jnp.where((lane & quarter) == 0, pltpu.roll(x, HEAD_DIM - quarter, 1), pltpu.roll(x, quarter, 1))
    return x * cos + partner * sin


def _attn_kernel(sink_ref, q_ref, kp_ref, kc_ref, kn_ref, vp_ref, vc_ref, vn_ref, kx_ref, vx_ref,
                 cp_ref, cc_ref, cn_ref, sp_ref, sc_ref, sn_ref, o_ref, *, nct, nt):
    i = pl.program_id(1)
    nt_dims = (((1,), (1,)), ((), ()))
    rows = ATTN_GROUP * TT
    neg = jnp.float32(-jnp.inf)

    def scores(q, k):
        return lax.dot_general(q, k.astype(BF16), nt_dims, preferred_element_type=F32) * ATTN_SCALE

    def pv(p, v):
        return jnp.dot(p.astype(BF16), v.astype(BF16), preferred_element_type=F32)

    def head_slice(ref, g):
        return ref[0, :, g * HEAD_DIM:(g + 1) * HEAD_DIM]

    def sink_col(g):
        return jnp.concatenate(
            [jnp.full((TT, 1), sink_ref[g * ATTN_GROUP + j], F32) for j in range(ATTN_GROUP)], axis=0)

    def q_rows(g, rope):
        parts = [q_ref[0, :, (g * ATTN_GROUP + j) * HEAD_DIM:(g * ATTN_GROUP + j + 1) * HEAD_DIM]
                 for j in range(ATTN_GROUP)]
        return jnp.concatenate([rope(p) for p in parts], axis=0).astype(BF16)

    def store(g, o, den):
        o = o / den
        for j in range(ATTN_GROUP):
            c0 = (g * ATTN_GROUP + j) * HEAD_DIM
            o_ref[0, :, c0:c0 + HEAD_DIM] = o[j * TT:(j + 1) * TT].astype(o_ref.dtype)

    rmax = lambda s: jnp.max(s, axis=-1, keepdims=True)
    rsum = lambda p: jnp.sum(p, axis=-1, keepdims=True)

    @pl.when(i >= nct)
    def _():
        cos_c, sin_c = cc_ref[...], sc_ref[...]
        row = lax.broadcasted_iota(jnp.int32, (rows, TT), 0) & (TT - 1)
        col = lax.broadcasted_iota(jnp.int32, (rows, TT), 1)
        keep_prev = col >= row + jnp.where(i > nct, 0, TT)
        keep_next = col <= row - jnp.where(i < nt - 1, 0, TT)
        for g in range(ATTN_KV):
            q = q_rows(g, lambda x: _rope(x, cos_c, sin_c))
            s_p = jnp.where(keep_prev, scores(q, _rope(head_slice(kp_ref, g), cp_ref[...], sp_ref[...])), neg)
            s_c = scores(q, _rope(head_slice(kc_ref, g), cos_c, sin_c))
            s_n = jnp.where(keep_next, scores(q, _rope(head_slice(kn_ref, g), cn_ref[...], sn_ref[...])), neg)
            s_x = scores(q, head_slice(kx_ref, g))
            snk = sink_col(g)
            halves = lambda t: [t[:, c0:c0 + TT] for c0 in range(0, t.shape[1], TT)]
            m_el = functools.reduce(jnp.maximum, [s_p, s_c, s_n] + halves(s_x))
            m = jnp.maximum(rmax(m_el), snk)
            p_p, p_c, p_n, p_x = jnp.exp(s_p - m), jnp.exp(s_c - m), jnp.exp(s_n - m), jnp.exp(s_x - m)
            den = rsum(functools.reduce(jnp.add, [p_p, p_c, p_n] + halves(p_x))) + jnp.exp(snk - m)
            o = (pv(p_p, head_slice(vp_ref, g)) + pv(p_c, head_slice(vc_ref, g)) + pv(p_n, head_slice(vn_ref, g))
                 + pv(p_x, head_slice(vx_ref, g)))
            store(g, o, den)

    @pl.when(i < nct)
    def _():
        for g in range(ATTN_KV):
            q = q_rows(g, lambda x: x)
            s_x = scores(q, head_slice(kx_ref, g))
            snk = sink_col(g)
            halves = lambda t: [t[:, c0:c0 + TT] for c0 in range(0, t.shape[1], TT)]
            m = jnp.maximum(rmax(functools.reduce(jnp.maximum, halves(s_x))), snk)
            p_x = jnp.exp(s_x - m)
            store(g, pv(p_x, head_slice(vx_ref, g)), rsum(functools.reduce(jnp.add, halves(p_x))) + jnp.exp(snk - m))


def _rope_tables(n_ctx, s):
    quarter = HEAD_DIM // 4
    inv_freq = ROPE_BASE ** (-jnp.arange(quarter, dtype=F32) / quarter)
    pos = jnp.arange(s)
    ang_r = (pos // GRID_W).astype(F32)[:, None] * inv_freq[None, :]
    ang_c = (pos % GRID_W).astype(F32)[:, None] * inv_freq[None, :]
    cos = jnp.concatenate([jnp.cos(ang_r), jnp.cos(ang_r), jnp.cos(ang_c), jnp.cos(ang_c)], axis=-1)
    sin = jnp.concatenate([-jnp.sin(ang_r), jnp.sin(ang_r), -jnp.sin(ang_c), jnp.sin(ang_c)], axis=-1)
    pad = jnp.zeros((n_ctx, HEAD_DIM), F32)
    return jnp.concatenate([pad, cos], axis=0), jnp.concatenate([pad, sin], axis=0)


def _attention(proj, sink, cos, sin, n_ctx, nct, nt):
    nb, tt, _ = proj.shape
    kvw = ATTN_KV * HEAD_DIM
    prev_i = lambda i: jnp.maximum(i - 1, 0)
    next_i = lambda i: jnp.minimum(i + 1, nt - 1)

    def kv(base, which):
        return pl.BlockSpec((1, TT, kvw), lambda b, i: (b, which(i), base // kvw))

    def tab(which):
        return pl.BlockSpec((TT, HEAD_DIM), lambda b, i: (which(i), 0))

    same = lambda i: i
    in_specs = [pl.BlockSpec(memory_space=pltpu.SMEM),
                pl.BlockSpec((1, TT, D), lambda b, i: (b, i, AB_Q // D)),
                kv(AB_K, prev_i), kv(AB_K, same), kv(AB_K, next_i),
                kv(AB_V, prev_i), kv(AB_V, same), kv(AB_V, next_i),
                pl.BlockSpec((1, n_ctx, kvw), lambda b, i: (b, 0, AB_K // kvw)),
                pl.BlockSpec((1, n_ctx, kvw), lambda b, i: (b, 0, AB_V // kvw)),
                tab(prev_i), tab(same), tab(next_i), tab(prev_i), tab(same), tab(next_i)]
    return pl.pallas_call(
        functools.partial(_attn_kernel, nct=nct, nt=nt),
        grid=(nb, nt), in_specs=in_specs,
        out_specs=pl.BlockSpec((1, TT, D), lambda b, i: (b, i, 0)),
        out_shape=jax.ShapeDtypeStruct((nb, tt, D), BF16),
        compiler_params=_cparams("parallel", "arbitrary"), name="attention",
    )(sink.astype(F32), proj, proj, proj, proj, proj, proj, proj, proj, proj, cos, cos, cos, sin, sin, sin)


def _conformer_kernel(a_ref, ap_ref, an_ref, b_ref, bp_ref, bn_ref, w_ref, cb_ref, ln_ref, o_ref, ext_ref, acc_ref,
                      *, nct, nt):
    i = pl.program_id(1)
    has_prev, has_next = _segment_edges(i, nct, nt)
    _fill_ext(ext_ref, a_ref[0] * _sigmoid(b_ref[0]), ap_ref[0] * _sigmoid(bp_ref[0]), an_ref[0] * _sigmoid(bn_ref[0]),
              has_prev, has_next)

    def cols(c, carry):
        c0 = pl.multiple_of(c * LANE, LANE)
        acc_ref[:, pl.ds(c0, LANE)] = _dwconv_cols_grouped(ext_ref, w_ref, cb_ref, CONF_KERNEL, CONF_PAD_L, c0, LANE)
        return carry

    lax.fori_loop(0, D // LANE, cols, 0)
    u = acc_ref[...]
    mu = jnp.mean(u, axis=-1, keepdims=True)
    uc = u - mu
    var = jnp.mean(uc * uc, axis=-1, keepdims=True)
    o_ref[0] = _silu(uc * lax.rsqrt(var + LN_EPS) * ln_ref[0:1, :] + ln_ref[1:2, :]).astype(o_ref.dtype)


def _conformer(proj, dw_w, dw_b, ln_g, ln_b, nct, nt):
    nb, tt, _ = proj.shape
    full = lambda shape: pl.BlockSpec(shape, lambda b, i: (0,) * len(shape))
    return pl.pallas_call(
        functools.partial(_conformer_kernel, nct=nct, nt=nt),
        grid=(nb, nt),
        in_specs=[*_tile_specs(D, CD_GLU_A // D, nt), *_tile_specs(D, CD_GLU_B // D, nt),
                  full((CONF_KERNEL, D)), full((1, D)), full((2, D))],
        out_specs=pl.BlockSpec((1, TT, D), lambda b, i: (b, i, 0)),
        out_shape=jax.ShapeDtypeStruct((nb, tt, D), BF16),
        scratch_shapes=[pltpu.VMEM((TT + 2 * HALO, D), F32), pltpu.VMEM((TT, D), F32)],
        compiler_params=_cparams("parallel", "arbitrary"), name="conformer",
    )(proj, proj, proj, proj, proj, proj, dw_w, dw_b[None, :], jnp.stack([ln_g, ln_b]))


def _rg_kernel(*refs, direction, final, nct, nt):
    if final:
        (x_ref, xp_ref, xn_ref, w_ref, cb_ref, gw_ref, gb_ref, lam_ref, hf_ref, gate_ref, o_ref,
         ext_ref, a_ref, b_ref, h_ref, hs_ref) = refs
    else:
        x_ref, xp_ref, xn_ref, w_ref, cb_ref, gw_ref, gb_ref, lam_ref, o_ref, ext_ref, a_ref, b_ref, h_ref, hs_ref = refs
    reverse = direction == 1
    i = pl.program_id(1)

    @pl.when(i == 0)
    def _():
        h_ref[...] = jnp.zeros_like(h_ref)

    has_prev, has_next = _segment_edges(_scan_tile_index(nct, nt, reverse)(i), nct, nt)
    _fill_ext(ext_ref, x_ref[0], xp_ref[0], xn_ref[0], has_prev, has_next)
    for blk in range(RG_BLOCKS):
        c0 = blk * RG_BLOCK
        xr = _dwconv_cols(ext_ref, w_ref, cb_ref, SHORT_CONV, SHORT_PAD_L, c0, RG_BLOCK)
        pre = jnp.dot(xr.astype(BF16), gw_ref[blk], preferred_element_type=F32) + gb_ref[blk:blk + 1, :]
        r = _sigmoid(pre[:, :RG_BLOCK])
        gate_i = _sigmoid(pre[:, RG_BLOCK:])
        log_a = -RG_C * r * _softplus(-lam_ref[:, c0:c0 + RG_BLOCK])
        a = jnp.exp(log_a)
        a_ref[:, c0:c0 + RG_BLOCK] = a
        b_ref[:, c0:c0 + RG_BLOCK] = jnp.sqrt(1.0 - a * a) * gate_i * xr

    def step(s, h):
        t = (TT - 1 - s) if reverse else s
        h = a_ref[pl.ds(t, 1), :] * h + b_ref[pl.ds(t, 1), :]
        hs_ref[pl.ds(t, 1), :] = h
        return h

    h_ref[...] = lax.fori_loop(0, TT, step, h_ref[...], unroll=SUBLANE)
    if final:
        o_ref[0] = ((hf_ref[0] + hs_ref[...]) * _gelu_tanh(gate_ref[0])).astype(o_ref.dtype)
    else:
        o_ref[0] = hs_ref[...]


def _rg_direction(proj, conv_w, conv_b, gate_w, gate_b, lam, direction, nct, nt, *, h_fwd=None):
    nb, tt, _ = proj.shape
    final = h_fwd is not None
    tile_of = _scan_tile_index(nct, nt, direction == 1)
    full = lambda shape: pl.BlockSpec(shape, lambda b, i: (0,) * len(shape))
    tok = lambda cb=0: pl.BlockSpec((1, TT, D), lambda bb, i: (bb, tile_of(i), cb))
    gw = jnp.transpose(gate_w[direction], (1, 2, 0, 3)).reshape(RG_BLOCKS, RG_BLOCK, 2 * RG_BLOCK).astype(BF16)
    gb = jnp.transpose(gate_b[direction].reshape(2, RG_BLOCKS, RG_BLOCK), (1, 0, 2)).reshape(RG_BLOCKS, 2 * RG_BLOCK)
    args = [proj, proj, proj, conv_w, conv_b[None, :], gw, gb.astype(F32), lam[direction][None, :].astype(F32)]
    in_specs = [*_tile_specs(D, CD_RX // D, nt, tile_of), full((SHORT_CONV, D)), full((1, D)),
                full((RG_BLOCKS, RG_BLOCK, 2 * RG_BLOCK)), full((RG_BLOCKS, 2 * RG_BLOCK)), full((1, D))]
    if final:
        args += [h_fwd, proj]
        in_specs += [tok(), tok(CD_RGATE // D)]
    return pl.pallas_call(
        functools.partial(_rg_kernel, direction=direction, final=final, nct=nct, nt=nt),
        grid=(nb, nt), in_specs=in_specs, out_specs=tok(),
        out_shape=jax.ShapeDtypeStruct((nb, tt, D), BF16 if final else F32),
        scratch_shapes=[pltpu.VMEM((TT + 2 * HALO, D), F32), pltpu.VMEM((TT, D), F32), pltpu.VMEM((TT, D), F32),
                        pltpu.VMEM((1, D), F32), pltpu.VMEM((TT, D), F32)],
        compiler_params=_cparams("parallel", "arbitrary"), name=f"rg_dir{direction}")(*args)


def _even_mixer(h, w_in, w_out, conv_w, conv_b, dt_bias, a_log, d_skip, norm_w, sink, cos, sin, n_ctx, nct, nt):
    nb, tt, _ = h.shape
    w = w_in
    w_perm = jnp.concatenate([w[:, 2048:4096], w[:, 0:2048], w[:, 5184:7232], w[:, 4096:5120], w[:, 7232:7744],
                              w[:, 7744:8256], w[:, 5120:5184], jnp.zeros((D, AB_N - 8256), w.dtype)], axis=1)
    proj = _matmul([h.reshape(nb * tt, D)], [w_perm.astype(BF16)], F32, "ab_in").reshape(nb, tt, AB_N)
    xs_c, bc_c, dt_sp = _ssd_prep(proj, conv_w, conv_b, dt_bias, nct, nt)
    y_fwd = _ssd_scan(xs_c, bc_c, dt_sp, a_log, 0, nct, nt, d_skip=d_skip)
    ssd_o = _ssd_scan(xs_c, bc_c, dt_sp, a_log, 1, nct, nt, y_in=y_fwd, proj=proj, norm_w=norm_w)
    att_o = _attention(proj, sink, cos, sin, n_ctx, nct, nt)
    wo = w_out.astype(BF16)
    return _matmul([ssd_o.reshape(nb * tt, D), att_o.reshape(nb * tt, D)], [wo[:D], wo[D:]], BF16, "ab_out")


def _odd_mixer(h, w_in, w_out, dw_w, dw_b, cln_g, cln_b, rconv_w, rconv_b, gate_w, gate_b, lam, nct, nt):
    nb, tt, _ = h.shape
    proj = _matmul([h.reshape(nb * tt, D)], [w_in.astype(BF16)], F32, "cd_in").reshape(nb, tt, CD_N)
    conf_o = _conformer(proj, dw_w, dw_b, cln_g, cln_b, nct, nt)
    h_fwd = _rg_direction(proj, rconv_w, rconv_b, gate_w, gate_b, lam, 0, nct, nt)
    rg_o = _rg_direction(proj, rconv_w, rconv_b, gate_w, gate_b, lam, 1, nct, nt, h_fwd=h_fwd)
    wo = w_out.astype(BF16)
    return _matmul([conf_o.reshape(nb * tt, D), rg_o.reshape(nb * tt, D)], [wo[:D], wo[D:]], BF16, "cd_out")


def kernel(x, c, ctx, c_ctx, mod_w, mod_b, ln_g, ln_b, ab_w_in, ab_w_out, ssd_conv_w, ssd_conv_b, ssd_dt_bias, ssd_a_log, ssd_d, ssd_norm_w, attn_sink, ffn_w1, ffn_w3, ffn_w2, cd_w_in, cd_w_out, conf_dw_w, conf_dw_b, conf_ln_g, conf_ln_b, rg_conv_w, rg_conv_b, rg_gate_w, rg_gate_b, rg_lambda, moe_router, moe_w1, moe_w3, moe_w2):
    nb, s, _ = x.shape
    n_ctx = ctx.shape[1]
    depth = mod_w.shape[0]
    assert s % TT == 0 and n_ctx % TT == 0 and n_ctx >= TT and x.shape[2] == D
    tt = n_ctx + s
    nct, nt = n_ctx // TT, tt // TT
    alpha = (2 * depth) ** 0.25

    modt = _modulation_tables(c, c_ctx, mod_w, mod_b)
    cos, sin = _rope_tables(n_ctx, s)
    xs = jnp.concatenate([ctx, x], axis=1)
    (h,) = _post(xs, n_ctx, nxt=(modt[0], 0, 1))
    for l in range(depth):
        j = l // 2
        if l % 2 == 0:
            mix = _even_mixer(h, ab_w_in[j], ab_w_out[j], ssd_conv_w[j], ssd_conv_b[j], ssd_dt_bias[j], ssd_a_log[j],
                              ssd_d[j], ssd_norm_w[j], attn_sink[j], cos, sin, n_ctx, nct, nt)
        else:
            mix = _odd_mixer(h, cd_w_in[j], cd_w_out[j], conf_dw_w[j], conf_dw_b[j], conf_ln_g[j], conf_ln_b[j],
                             rg_conv_w[j], rg_conv_b[j], rg_gate_w[j], rg_gate_b[j], rg_lambda[j], nct, nt)
        ln0 = jnp.stack([ln_g[l, 0], ln_b[l, 0]])
        ln1 = jnp.stack([ln_g[l, 1], ln_b[l, 1]])
        res = (mix.reshape(nb, tt, D), modt[l], 2, ln0, alpha)
        hf = h.reshape(nb * tt, D)
        if l % 2 == 0:
            xs, h = _post(xs, n_ctx, res=res, nxt=(modt[l], 3, 4))
            hf = h.reshape(nb * tt, D)
            act = _swiglu_up(hf, ffn_w1[j].astype(BF16), ffn_w3[j].astype(BF16), "ffn_up")
            f = _matmul([act], [ffn_w2[j].astype(BF16)], BF16, "ffn_down")
        else:
            wr = jnp.zeros((D, LANE), F32).at[:, :N_EXPERTS].set(moe_router[j])
            xs, h, comb = _post(xs, n_ctx, res=res, nxt=(modt[l], 3, 4), router=wr)
            hf = h.reshape(nb * tt, D)
            f = _moe_sparse(hf, comb.reshape(nb * tt, LANE), moe_w1, moe_w3, moe_w2, j)
        res = (f.reshape(nb, tt, D), modt[l], 5, ln1, alpha)
        if l + 1 < depth:
            xs, h = _post(xs, n_ctx, res=res, nxt=(modt[l + 1], 0, 1))
        else:
            (out,) = _post(xs, n_ctx, res=res, latent_only=True)
    return out
```

```python
import functools
import math

import jax
import jax.numpy as jnp
import numpy as np
from jax import lax
from jax.experimental import pallas as pl
from jax.experimental.pallas import tpu as pltpu

F32 = jnp.float32
BF16 = jnp.bfloat16
HIGHEST = lax.Precision.HIGHEST

LANE = 128
SUBLANE = 8
VMEM_BYTES = 64 * 1024 * 1024
VMEM_LIMIT = VMEM_BYTES * 7 // 8
MATMUL_VMEM_BUDGET = VMEM_BYTES * 5 // 8

D = 2048
GRID_W = 64
SSD_HEAD_DIM = 64
SSD_HEADS = D // SSD_HEAD_DIM
SSD_GROUPS = 4
SSD_STATE = 128
SSD_GROUP_HEADS = SSD_HEADS // SSD_GROUPS
SHORT_CONV = 4
SHORT_PAD_L = 2
HEAD_DIM = 128
ATTN_HEADS = D // HEAD_DIM
ATTN_KV = ATTN_HEADS // 4
ATTN_GROUP = ATTN_HEADS // ATTN_KV
WINDOW = 128
ATTN_SCALE = HEAD_DIM ** -0.5
ROPE_BASE = 10000.0
CONF_KERNEL = 31
CONF_PAD_L = 15
RG_BLOCK = 128
RG_BLOCKS = D // RG_BLOCK
RG_C = 8.0
N_EXPERTS = 8
ROUTE_I1, ROUTE_I2 = N_EXPERTS, N_EXPERTS + 1
LN_EPS = 1e-5

TT = 128
HALO = 16

AB_XS, AB_Z, AB_Q, AB_BC, AB_K, AB_V, AB_DT = 0, 2048, 4096, 6144, 7168, 7680, 8192
AB_N = 8448
CD_GLU_A, CD_GLU_B, CD_RX, CD_RGATE = 0, 2048, 4096, 6144
CD_N = 8192


def _cparams(*sem):
    return pltpu.CompilerParams(dimension_semantics=sem, vmem_limit_bytes=VMEM_LIMIT)


def _sigmoid(x):
    return 0.5 * (1.0 + jnp.tanh(0.5 * x))


def _silu(x):
    return x * _sigmoid(x)


def _softplus(x):
    return jnp.maximum(x, 0.0) + jnp.log1p(jnp.exp(-jnp.abs(x)))


def _gelu_tanh(x):
    return 0.5 * x * (1.0 + jnp.tanh(math.sqrt(2.0 / math.pi) * (x + 0.044715 * (x * x * x))))


def _mod_kernel(c_ref, w_ref, b_ref, o_ref):
    s = _silu(c_ref[...])
    o_ref[0] = jnp.dot(s, w_ref[0], preferred_element_type=F32, precision=HIGHEST) + b_ref[0]


def _modulation_tables(c, c_ctx, mod_w, mod_b):
    depth, _, n6 = mod_w.shape
    nb = c.shape[0]
    rows = -(-(nb + 1) // SUBLANE) * SUBLANE
    cc = jnp.zeros((rows, D), F32).at[:nb].set(c).at[nb].set(c_ctx)
    tn = 1024
    out = pl.pallas_call(
        _mod_kernel,
        grid=(depth, n6 // tn),
        in_specs=[pl.BlockSpec((rows, D), lambda l, j: (0, 0)),
                  pl.BlockSpec((1, D, tn), lambda l, j: (l, 0, j)),
                  pl.BlockSpec((1, 1, tn), lambda l, j: (l, 0, j))],
        out_specs=pl.BlockSpec((1, rows, tn), lambda l, j: (l, 0, j)),
        out_shape=jax.ShapeDtypeStruct((depth, rows, n6), F32),
        compiler_params=_cparams("arbitrary", "arbitrary"),
        name="modulation",
    )(cc, mod_w, mod_b.reshape(depth, 1, n6))
    lat = out[:, :nb].reshape(depth, nb, 1, 6, D)
    ctx = jnp.broadcast_to(out[:, nb].reshape(depth, 1, 1, 6, D), (depth, nb, 1, 6, D))
    return jnp.concatenate([ctx, lat], axis=2)


def _post_kernel(*refs, alpha, gate_row, mod_rows, has_res, has_h, has_router):
    it = iter(refs)
    x_ref = next(it)
    if has_res:
        m_ref, modg_ref, ln_ref = next(it), next(it), next(it)
    if has_h:
        modn_ref = next(it)
    if has_router:
        wr_ref = next(it)
    if has_res:
        xo_ref = next(it)
    if has_h:
        h_ref = next(it)
    if has_router:
        comb_ref = next(it)

    x = x_ref[0]
    if has_res:
        g = modg_ref[0, 0, gate_row:gate_row + 1, :]
        xf = alpha * x + g * m_ref[0].astype(F32)
        mu = jnp.mean(xf, axis=-1, keepdims=True)
        xc = xf - mu
        var = jnp.mean(xc * xc, axis=-1, keepdims=True)
        x = xc * lax.rsqrt(var + LN_EPS) * ln_ref[0:1, :] + ln_ref[1:2, :]
        xo_ref[0] = x
    if has_h:
        sh = modn_ref[0, 0, mod_rows[0]:mod_rows[0] + 1, :]
        sc = modn_ref[0, 0, mod_rows[1]:mod_rows[1] + 1, :]
        h = x * (1.0 + sc) + sh
        h_ref[0] = h.astype(BF16)
        if has_router:
            logits = jnp.dot(h, wr_ref[...], preferred_element_type=F32, precision=HIGHEST)
            lane = lax.broadcasted_iota(jnp.int32, logits.shape, 1).astype(F32)
            neg = jnp.float32(-jnp.inf)
            lg = jnp.where(lane < N_EXPERTS, logits, neg)
            m1 = jnp.max(lg, axis=-1, keepdims=True)
            i1 = jnp.min(jnp.where(lg == m1, lane, float(LANE)), axis=-1, keepdims=True)
            lg2 = jnp.where(lane == i1, neg, lg)
            m2 = jnp.max(lg2, axis=-1, keepdims=True)
            i2 = jnp.min(jnp.where(lg2 == m2, lane, float(LANE)), axis=-1, keepdims=True)
            e2 = jnp.exp(m2 - m1)
            den = 1.0 + e2
            gates = jnp.where(lane == i1, 1.0 / den, 0.0) + jnp.where(lane == i2, e2 / den, 0.0)
            comb_ref[0] = jnp.where(lane == ROUTE_I1, i1, jnp.where(lane == ROUTE_I2, i2, gates))


def _post(x, n_ctx, *, res=None, nxt=None, router=None, latent_only=False):
    nb, tt, _ = x.shape
    tm = 256 if n_ctx % 256 == 0 else TT
    nct = n_ctx // tm
    first = nct if latent_only else 0
    tt_out = tt - first * tm
    tok_in = pl.BlockSpec((1, tm, D), lambda b, i: (b, i + first, 0))
    tok = pl.BlockSpec((1, tm, D), lambda b, i: (b, i, 0))
    modspec = pl.BlockSpec((1, 1, 6, D), lambda b, i: (b, jnp.where(i + first >= nct, 1, 0), 0, 0))
    args, in_specs, out_shape, out_specs = [x], [tok_in], [], []
    if res is not None:
        m, modt, gate_row, ln_gb, alpha = res
        args += [m, modt, ln_gb]
        in_specs += [tok_in, modspec, pl.BlockSpec((2, D), lambda b, i: (0, 0))]
        out_shape.append(jax.ShapeDtypeStruct((nb, tt_out, D), F32))
        out_specs.append(tok)
    else:
        gate_row, alpha = 0, 1.0
    if nxt is not None:
        args.append(nxt[0])
        in_specs.append(modspec)
        out_shape.append(jax.ShapeDtypeStruct((nb, tt_out, D), BF16))
        out_specs.append(tok)
    if router is not None:
        args.append(router)
        in_specs.append(pl.BlockSpec((D, LANE), lambda b, i: (0, 0)))
        out_shape.append(jax.ShapeDtypeStruct((nb, tt_out, LANE), F32))
        out_specs.append(pl.BlockSpec((1, tm, LANE), lambda b, i: (b, i, 0)))
    kern = functools.partial(_post_kernel, alpha=alpha, gate_row=gate_row,
                             mod_rows=(nxt[1], nxt[2]) if nxt is not None else (0, 0),
                             has_res=res is not None, has_h=nxt is not None, has_router=router is not None)
    return pl.pallas_call(kern, grid=(nb, tt_out // tm), in_specs=in_specs, out_specs=out_specs, out_shape=out_shape,
                          compiler_params=_cparams("parallel", "arbitrary"), name="post")(*args)


def _mm_kernel(*refs, n_a):
    o_ref = refs[-1]
    acc = jnp.dot(refs[0][...], refs[n_a][...], preferred_element_type=F32)
    for t in range(1, n_a):
        acc = acc + jnp.dot(refs[t][...], refs[n_a + t][...], preferred_element_type=F32)
    o_ref[...] = acc.astype(o_ref.dtype)


def _glu_kernel(a_ref, w1_ref, w3_ref, o_ref):
    a = a_ref[...]
    g = jnp.dot(a, w1_ref[...], preferred_element_type=F32)
    u = jnp.dot(a, w3_ref[...], preferred_element_type=F32)
    o_ref[...] = (_silu(g) * u).astype(o_ref.dtype)


def _pick_tiles(m, n, k_total, n_w, out_bytes):
    best = None
    for tm in (1024, 512, 256, 128):
        if m % tm:
            continue
        for tn in (1024, 768, 512, 256, 128):
            if n % tn:
                continue
            need = 2 * (tm * k_total * 2 + k_total * tn * 2 * n_w + tm * tn * out_bytes)
            if need <= MATMUL_VMEM_BUDGET and (best is None or tm * tn > best[0] * best[1]):
                best = (tm, tn)
    assert best is not None, (m, n, k_total)
    return best


def _matmul(a_list, w_list, out_dtype, name):
    m, n = a_list[0].shape[0], w_list[0].shape[1]
    k_total = sum(a.shape[1] for a in a_list)
    tm, tn = _pick_tiles(m, n, k_total, 1, jnp.dtype(out_dtype).itemsize)
    in_specs = ([pl.BlockSpec((tm, a.shape[1]), lambda j, i: (i, 0)) for a in a_list]
                + [pl.BlockSpec((w.shape[0], tn), lambda j, i: (0, j)) for w in w_list])
    return pl.pallas_call(
        functools.partial(_mm_kernel, n_a=len(a_list)),
        grid=(n // tn, m // tm), in_specs=in_specs,
        out_specs=pl.BlockSpec((tm, tn), lambda j, i: (i, j)),
        out_shape=jax.ShapeDtypeStruct((m, n), out_dtype),
        compiler_params=_cparams("parallel", "arbitrary"), name=name)(*a_list, *w_list)


def _swiglu_up(a, w1, w3, name):
    m, k = a.shape
    n = w1.shape[1]
    tm, tn = _pick_tiles(m, n, k, 2, 2)
    return pl.pallas_call(
        _glu_kernel, grid=(n // tn, m // tm),
        in_specs=[pl.BlockSpec((tm, k), lambda j, i: (i, 0)),
                  pl.BlockSpec((k, tn), lambda j, i: (0, j)),
                  pl.BlockSpec((k, tn), lambda j, i: (0, j))],
        out_specs=pl.BlockSpec((tm, tn), lambda j, i: (i, j)),
        out_shape=jax.ShapeDtypeStruct((m, n), BF16),
        compiler_params=_cparams("parallel", "arbitrary"), name=name)(a, w1, w3)


MOE_SUB = 512
MOE_ROWS = 512
MOE_ALIGN = 16
MOE_BLK = 64
MOE_CAT = -(-(2 * MOE_SUB + N_EXPERTS * (MOE_ALIGN - 1 + MOE_BLK - 1)) // LANE) * LANE


def _moe_rank_kernel(route_ref, tri_ref, rank_ref, cnt_ref):
    route = route_ref[...]
    lane = lax.broadcasted_iota(jnp.int32, route.shape, 1).astype(F32)
    onehot = jnp.where(lane == route[:, ROUTE_I1:ROUTE_I1 + 1], 1.0,
                       jnp.where(lane == route[:, ROUTE_I2:ROUTE_I2 + 1], 1.0, 0.0))
    rank_ref[...] = jnp.dot(tri_ref[...], onehot.astype(BF16), preferred_element_type=F32)
    cnt_ref[0] = jnp.broadcast_to(jnp.sum(onehot, axis=0, keepdims=True), (SUBLANE, LANE))


def _moe_plan(cnt, r_tiles):
    c16 = -(-cnt // MOE_ALIGN) * MOE_ALIGN
    nblk = -(-c16 // MOE_BLK)
    locw = nblk * MOE_BLK
    loc = jnp.cumsum(locw, axis=1) - locw
    tot = jnp.sum(c16, axis=0)
    region = -(-(tot + MOE_BLK) // MOE_ROWS) * MOE_ROWS
    base = jnp.cumsum(region) - region
    off = base[None, :] + jnp.cumsum(c16, axis=0) - c16
    ends = jnp.cumsum(region // MOE_ROWS)
    tile_expert = jnp.minimum(jnp.searchsorted(ends, jnp.arange(r_tiles, dtype=jnp.int32), side="right"),
                              N_EXPERTS - 1).astype(jnp.int32)
    flat = lambda a: a.reshape(-1).astype(jnp.int32)
    return flat(off), flat(nblk), flat(loc), tile_expert, ends[-1:].astype(jnp.int32)


def _moe_positions(route, rank, loc_ref, s):
    lane = lax.broadcasted_iota(jnp.int32, route.shape, 1)
    locv = jnp.zeros(route.shape, F32)
    for e in range(N_EXPERTS):
        locv = jnp.where(lane == e, loc_ref[s * N_EXPERTS + e].astype(F32), locv)
    pos = rank + locv
    lanef = lane.astype(F32)
    sel1 = lanef == route[:, ROUTE_I1:ROUTE_I1 + 1]
    sel2 = lanef == route[:, ROUTE_I2:ROUTE_I2 + 1]
    pick = lambda sel, v: jnp.sum(jnp.where(sel, v, 0.0), axis=-1, keepdims=True)
    return pick(sel1, pos), pick(sel2, pos), pick(sel1, route), pick(sel2, route)


def _moe_segment_copies(nblk_ref, step, make_copies, phase):
    for e in range(N_EXPERTS):
        def body(rb, carry, e=e):
            for cp in make_copies(step, e, rb):
                getattr(cp, phase)()
            return carry

        lax.fori_loop(0, nblk_ref[step * N_EXPERTS + e], body, 0)


def _moe_compact_kernel(off_ref, nblk_ref, loc_ref, h_ref, route_ref, rank_ref, xz_ref, gz_ref, xs_ref, gs_ref,
                        p_ref, xcat_ref, gcat_ref, sem, *, n_sub):
    del xz_ref, gz_ref
    s = pl.program_id(0)
    slot = s & 1
    lp0, lp1, g0, g1 = _moe_positions(route_ref[...], rank_ref[...], loc_ref, s)
    lane = lax.broadcasted_iota(jnp.int32, (MOE_SUB, LANE), 1)
    packed = jnp.where(lane == 0, lp0, jnp.where(lane == 1, lp1, jnp.where(lane == 2, g0, jnp.where(lane == 3, g1, 0.0))))
    pt = packed.T
    lp0r, lp1r, g0r, g1r = pt[0:1], pt[1:2], pt[2:3], pt[3:4]
    for c in range(MOE_CAT // LANE):
        r = (lax.broadcasted_iota(jnp.int32, (LANE, MOE_SUB), 0) + c * LANE).astype(F32)
        m0 = r == lp0r
        m1 = r == lp1r
        p_ref[c * LANE:(c + 1) * LANE, :] = jnp.where(m0, 1.0, jnp.where(m1, 1.0, 0.0)).astype(BF16)
        gate = jnp.sum(jnp.where(m0, g0r, 0.0) + jnp.where(m1, g1r, 0.0), axis=-1, keepdims=True)
        gcat_ref[slot, c * LANE:(c + 1) * LANE, :] = jnp.broadcast_to(gate, (LANE, LANE))
    xcat_ref[slot] = jnp.dot(p_ref[...], h_ref[...], preferred_element_type=F32).astype(BF16)

    def copies(step, e, rb):
        sl = step & 1
        src = pl.multiple_of(loc_ref[step * N_EXPERTS + e] + rb * MOE_BLK, MOE_BLK)
        dst = pl.multiple_of(off_ref[step * N_EXPERTS + e] + rb * MOE_BLK, MOE_ALIGN)
        return (pltpu.make_async_copy(xcat_ref.at[sl, pl.ds(src, MOE_BLK)], xs_ref.at[pl.ds(dst, MOE_BLK)], sem.at[sl, 0]),
                pltpu.make_async_copy(gcat_ref.at[sl, pl.ds(src, MOE_BLK)], gs_ref.at[pl.ds(dst, MOE_BLK)], sem.at[sl, 1]))

    @pl.when(s > 0)
    def _():
        _moe_segment_copies(nblk_ref, s - 1, copies, "wait")

    _moe_segment_copies(nblk_ref, s, copies, "start")

    @pl.when(s == n_sub - 1)
    def _():
        _moe_segment_copies(nblk_ref, s, copies, "wait")


def _moe_uncompact_kernel(off_ref, nblk_ref, loc_ref, route_ref, rank_ref, ys_ref, o_ref, pt_ref, ycat_ref, sem,
                          *, n_sub):
    s = pl.program_id(0)
    slot = s & 1

    def copies(step, e, rb):
        sl = step & 1
        src = pl.multiple_of(off_ref[step * N_EXPERTS + e] + rb * MOE_BLK, MOE_ALIGN)
        dst = pl.multiple_of(loc_ref[step * N_EXPERTS + e] + rb * MOE_BLK, MOE_BLK)
        return (pltpu.make_async_copy(ys_ref.at[pl.ds(src, MOE_BLK)], ycat_ref.at[sl, pl.ds(dst, MOE_BLK)], sem.at[sl]),)

    @pl.when(s == 0)
    def _():
        ycat_ref[...] = jnp.zeros_like(ycat_ref)
        _moe_segment_copies(nblk_ref, s, copies, "start")

    @pl.when(s + 1 < n_sub)
    def _():
        _moe_segment_copies(nblk_ref, s + 1, copies, "start")

    lp0, lp1, _, _ = _moe_positions(route_ref[...], rank_ref[...], loc_ref, s)
    for c in range(MOE_CAT // LANE):
        col = (lax.broadcasted_iota(jnp.int32, (MOE_SUB, LANE), 1) + c * LANE).astype(F32)
        pt_ref[:, c * LANE:(c + 1) * LANE] = jnp.where(col == lp0, 1.0, jnp.where(col == lp1, 1.0, 0.0)).astype(BF16)
    _moe_segment_copies(nblk_ref, s, copies, "wait")
    o_ref[...] = jnp.dot(pt_ref[...], ycat_ref[slot], preferred_element_type=F32).astype(o_ref.dtype)


def _expert_block_is_new(te_ref, i):
    return jnp.logical_or(i == 0, te_ref[i] != te_ref[jnp.maximum(i - 1, 0)])


def _moe_up_kernel(te_ref, nu_ref, x_ref, w1_ref, w3_ref, o_ref, w1b_ref, w3b_ref):
    i = pl.program_id(1)

    @pl.when(_expert_block_is_new(te_ref, i))
    def _():
        w1b_ref[...] = w1_ref[0, 0].astype(BF16)
        w3b_ref[...] = w3_ref[0, 0].astype(BF16)

    @pl.when(i < nu_ref[0])
    def _():
        a = x_ref[...]
        g = jnp.dot(a, w1b_ref[...], preferred_element_type=F32)
        u = jnp.dot(a, w3b_ref[...], preferred_element_type=F32)
        o_ref[...] = (_silu(g) * u).astype(o_ref.dtype)

    @pl.when(i >= nu_ref[0])
    def _():
        o_ref[...] = jnp.zeros_like(o_ref)


def _moe_down_kernel(te_ref, nu_ref, a_ref, w2_ref, g_ref, o_ref, w2b_ref):
    i = pl.program_id(1)

    @pl.when(_expert_block_is_new(te_ref, i))
    def _():
        w2b_ref[...] = w2_ref[0, 0].astype(BF16)

    @pl.when(i < nu_ref[0])
    def _():
        y = jnp.dot(a_ref[...], w2b_ref[...], preferred_element_type=F32)
        o_ref[...] = (g_ref[:, 0:1] * y).astype(o_ref.dtype)

    @pl.when(i >= nu_ref[0])
    def _():
        o_ref[...] = jnp.zeros_like(o_ref)


def _moe_sparse(h, route, w1, w3, w2, layer):
    t, k = h.shape
    _, ne, _, f = w1.shape
    assert t % MOE_SUB == 0 and ne == N_EXPERTS
    n_sub = t // MOE_SUB
    r_tiles = -(-(2 * t + n_sub * ne * (MOE_ALIGN - 1) + ne * (MOE_BLK + MOE_ROWS - 1)) // MOE_ROWS)
    r_rows = r_tiles * MOE_ROWS
    sub = lambda w: pl.BlockSpec((MOE_SUB, w), lambda s, *_: (s, 0))
    any_spec = pl.BlockSpec(memory_space=pl.ANY)

    tri = jnp.asarray(np.tril(np.ones((MOE_SUB, MOE_SUB), np.float32), -1), BF16)
    rank, cnt = pl.pallas_call(
        _moe_rank_kernel, grid=(n_sub,),
        in_specs=[sub(LANE), pl.BlockSpec((MOE_SUB, MOE_SUB), lambda s: (0, 0))],
        out_specs=[sub(LANE), pl.BlockSpec((1, SUBLANE, LANE), lambda s: (s, 0, 0))],
        out_shape=[jax.ShapeDtypeStruct((t, LANE), F32), jax.ShapeDtypeStruct((n_sub, SUBLANE, LANE), F32)],
        compiler_params=_cparams("parallel"), name="moe_rank")(route, tri)
    off, nblk, loc, tile_expert, n_used = _moe_plan(cnt[:, 0, :ne].astype(jnp.int32), r_tiles)

    x_sorted, gate_sorted = pl.pallas_call(
        functools.partial(_moe_compact_kernel, n_sub=n_sub),
        grid_spec=pltpu.PrefetchScalarGridSpec(
            num_scalar_prefetch=3, grid=(n_sub,),
            in_specs=[sub(k), sub(LANE), sub(LANE), any_spec, any_spec],
            out_specs=[any_spec, any_spec],
            scratch_shapes=[pltpu.VMEM((MOE_CAT, MOE_SUB), BF16), pltpu.VMEM((2, MOE_CAT, k), BF16),
                            pltpu.VMEM((2, MOE_CAT, LANE), F32), pltpu.SemaphoreType.DMA((2, 2))]),
        out_shape=[jax.ShapeDtypeStruct((r_rows, k), BF16), jax.ShapeDtypeStruct((r_rows, LANE), F32)],
        input_output_aliases={6: 0, 7: 1},
        compiler_params=_cparams("arbitrary"), name="moe_compact",
    )(off, nblk, loc, h, route, rank, jnp.zeros((r_rows, k), BF16), jnp.zeros((r_rows, LANE), F32))

    row_tile = lambda j, i, te, nu: jnp.minimum(i, nu[0] - 1)
    tn = 1024
    act = pl.pallas_call(
        _moe_up_kernel,
        grid_spec=pltpu.PrefetchScalarGridSpec(
            num_scalar_prefetch=2, grid=(f // tn, r_tiles),
            in_specs=[pl.BlockSpec((MOE_ROWS, k), lambda j, i, te, nu: (row_tile(j, i, te, nu), 0)),
                      pl.BlockSpec((1, 1, k, tn), lambda j, i, te, nu: (layer, te[i], 0, j)),
                      pl.BlockSpec((1, 1, k, tn), lambda j, i, te, nu: (layer, te[i], 0, j))],
            out_specs=pl.BlockSpec((MOE_ROWS, tn), lambda j, i, te, nu: (i, j)),
            scratch_shapes=[pltpu.VMEM((k, tn), BF16), pltpu.VMEM((k, tn), BF16)]),
        out_shape=jax.ShapeDtypeStruct((r_rows, f), BF16),
        compiler_params=_cparams("parallel", "arbitrary"), name="moe_up")(tile_expert, n_used, x_sorted, w1, w3)
    y_sorted = pl.pallas_call(
        _moe_down_kernel,
        grid_spec=pltpu.PrefetchScalarGridSpec(
            num_scalar_prefetch=2, grid=(k // tn, r_tiles),
            in_specs=[pl.BlockSpec((MOE_ROWS, f), lambda j, i, te, nu: (row_tile(j, i, te, nu), 0)),
                      pl.BlockSpec((1, 1, f, tn), lambda j, i, te, nu: (layer, te[i], 0, j)),
                      pl.BlockSpec((MOE_ROWS, LANE), lambda j, i, te, nu: (row_tile(j, i, te, nu), 0))],
            out_specs=pl.BlockSpec((MOE_ROWS, tn), lambda j, i, te, nu: (i, j)),
            scratch_shapes=[pltpu.VMEM((f, tn), BF16)]),
        out_shape=jax.ShapeDtypeStruct((r_rows, k), BF16),
        compiler_params=_cparams("parallel", "arbitrary"), name="moe_down")(tile_expert, n_used, act, w2, gate_sorted)

    return pl.pallas_call(
        functools.partial(_moe_uncompact_kernel, n_sub=n_sub),
        grid_spec=pltpu.PrefetchScalarGridSpec(
            num_scalar_prefetch=3, grid=(n_sub,),
            in_specs=[sub(LANE), sub(LANE), any_spec],
            out_specs=sub(k),
            scratch_shapes=[pltpu.VMEM((MOE_SUB, MOE_CAT), BF16), pltpu.VMEM((2, MOE_CAT, k), BF16),
                            pltpu.SemaphoreType.DMA((2,))]),
        out_shape=jax.ShapeDtypeStruct((t, k), BF16),
        compiler_params=_cparams("arbitrary"), name="moe_uncompact")(off, nblk, loc, route, rank, y_sorted)


def _tile_specs(width, col_block, n_tiles, tile_of=lambda i: i):
    per = TT // HALO
    last = n_tiles * per - 1
    cur = pl.BlockSpec((1, TT, width), lambda b, i: (b, tile_of(i), col_block))
    prev = pl.BlockSpec((1, HALO, width), lambda b, i: (b, jnp.maximum(tile_of(i) * per - 1, 0), col_block))
    nxt = pl.BlockSpec((1, HALO, width), lambda b, i: (b, jnp.minimum((tile_of(i) + 1) * per, last), col_block))
    return cur, prev, nxt


def _segment_edges(i, nct, nt):
    has_prev = jnp.logical_and(i != 0, i != nct)
    has_next = jnp.logical_and(i != nct - 1, i != nt - 1)
    return has_prev, has_next


def _fill_ext(ext_ref, cur, prev, nxt, has_prev, has_next):
    ext_ref[0:HALO, :] = jnp.where(has_prev, prev, 0.0)
    ext_ref[HALO:HALO + TT, :] = cur
    ext_ref[HALO + TT:HALO + TT + HALO, :] = jnp.where(has_next, nxt, 0.0)


def _dwconv_cols(ext_ref, w_ref, b_ref, taps, pad_l, c0, width):
    acc = jnp.broadcast_to(b_ref[:, pl.ds(c0, width)], (TT, width))
    for k in range(taps):
        off = HALO + k - pad_l
        acc = acc + ext_ref[off:off + TT, pl.ds(c0, width)] * w_ref[k:k + 1, pl.ds(c0, width)]
    return acc


def _dwconv_cols_grouped(ext_ref, w_ref, b_ref, taps, pad_l, c0, width):
    acc = jnp.broadcast_to(b_ref[:, pl.ds(c0, width)], (TT, width))
    rows = TT + SUBLANE
    for r in range(SUBLANE):
        part = None
        for k in range(taps):
            off = HALO + k - pad_l
            if off % SUBLANE != r:
                continue
            term = ext_ref[off - r:off - r + rows, pl.ds(c0, width)] * w_ref[k:k + 1, pl.ds(c0, width)]
            part = term if part is None else part + term
        if part is not None:
            acc = acc + part[r:r + TT]
    return acc


def _scan_tile_index(nct, nt, reverse):
    if not reverse:
        return lambda i: i
    return lambda i: jnp.where(i < nct, nct - 1 - i, nt + nct - 1 - i)


CONV_COLS = 256


def _ssd_prep_kernel(xs_ref, xsp_ref, xsn_ref, bc_ref, bcp_ref, bcn_ref, dt_ref, wx_ref, bx_ref, wb_ref, bb_ref,
                     dtb_ref, xo_ref, bo_ref, dto_ref, extx_ref, extb_ref, *, nct, nt):
    i = pl.program_id(1)
    has_prev, has_next = _segment_edges(i, nct, nt)
    _fill_ext(extx_ref, xs_ref[0], xsp_ref[0], xsn_ref[0], has_prev, has_next)
    _fill_ext(extb_ref, bc_ref[0], bcp_ref[0], bcn_ref[0], has_prev, has_next)
    for c0 in range(0, D, CONV_COLS):
        xo_ref[0, :, c0:c0 + CONV_COLS] = _silu(_dwconv_cols(extx_ref, wx_ref, bx_ref, SHORT_CONV, SHORT_PAD_L, c0, CONV_COLS))
    for c0 in range(0, 2 * SSD_GROUPS * SSD_STATE, CONV_COLS):
        bo_ref[0, :, c0:c0 + CONV_COLS] = _silu(_dwconv_cols(extb_ref, wb_ref, bb_ref, SHORT_CONV, SHORT_PAD_L, c0, CONV_COLS))
    dto_ref[0] = _softplus(dt_ref[0] + dtb_ref[...])


def _ssd_prep(proj, conv_w, conv_b, dt_bias, nct, nt):
    nb, tt, _ = proj.shape
    nbc = 2 * SSD_GROUPS * SSD_STATE
    xs_specs = _tile_specs(D, AB_XS // D, nt)
    bc_specs = _tile_specs(nbc, AB_BC // nbc, nt)
    full = lambda shape: pl.BlockSpec(shape, lambda b, i: (0,) * len(shape))
    dtb = jnp.zeros((1, LANE), F32).at[0, :2 * SSD_HEADS].set(dt_bias.reshape(-1))
    return pl.pallas_call(
        functools.partial(_ssd_prep_kernel, nct=nct, nt=nt),
        grid=(nb, nt),
        in_specs=[*xs_specs, *bc_specs,
                  pl.BlockSpec((1, TT, LANE), lambda b, i: (b, i, AB_DT // LANE)),
                  full((SHORT_CONV, D)), full((1, D)), full((SHORT_CONV, nbc)), full((1, nbc)), full((1, LANE))],
        out_specs=[pl.BlockSpec((1, TT, D), lambda b, i: (b, i, 0)),
                   pl.BlockSpec((1, TT, nbc), lambda b, i: (b, i, 0)),
                   pl.BlockSpec((1, TT, LANE), lambda b, i: (b, i, 0))],
        out_shape=[jax.ShapeDtypeStruct((nb, tt, D), F32), jax.ShapeDtypeStruct((nb, tt, nbc), F32),
                   jax.ShapeDtypeStruct((nb, tt, LANE), F32)],
        scratch_shapes=[pltpu.VMEM((TT + 2 * HALO, D), F32), pltpu.VMEM((TT + 2 * HALO, nbc), F32)],
        compiler_params=_cparams("parallel", "arbitrary"), name="ssd_prep",
    )(proj, proj, proj, proj, proj, proj, proj, conv_w[:, :D], conv_b[None, :D], conv_w[:, D:], conv_b[None, D:], dtb)


def _ssd_scan_kernel(*refs, direction, final):
    if final:
        xs_ref, bc_ref, dt_ref, aneg_ref, tri_ref, yin_ref, z_ref, nw_ref, o_ref, h_ref, y_ref = refs
    else:
        xs_ref, bc_ref, dt_ref, aneg_ref, tri_ref, dsk_ref, o_ref, h_ref = refs
    forward = direction == 0

    @pl.when(pl.program_id(1) == 0)
    def _():
        h_ref[...] = jnp.zeros_like(h_ref)

    dt = dt_ref[0]
    d_a = dt * aneg_ref[...]
    a_cs = jnp.dot(tri_ref[...], d_a, preferred_element_type=F32, precision=HIGHEST)
    end = TT - 1 if forward else 0
    tot = a_cs[end:end + 1, :]
    w_end = dt * jnp.exp(tot - a_cs)
    e_in = jnp.exp(a_cs)
    e_tot = jnp.exp(tot)
    a_cs_t, w_end_t, dt_t = a_cs.T, w_end.T, dt.T
    row = lax.broadcasted_iota(jnp.int32, (TT, TT), 0)
    col = lax.broadcasted_iota(jnp.int32, (TT, TT), 1)
    causal = (row >= col) if forward else (row <= col)
    low = col < SSD_HEAD_DIM
    neg = jnp.float32(-jnp.inf)
    ns = SSD_GROUPS * SSD_STATE
    pair_w = 2 * SSD_HEAD_DIM

    for g in range(SSD_GROUPS):
        b_g = bc_ref[0, :, g * SSD_STATE:(g + 1) * SSD_STATE]
        c_g = bc_ref[0, :, ns + g * SSD_STATE:ns + (g + 1) * SSD_STATE].astype(BF16)
        scores = lax.dot_general(c_g, b_g.astype(BF16), (((1,), (1,)), ((), ())), preferred_element_type=F32)
        b_t = b_g.T
        gw = SSD_GROUP_HEADS * SSD_HEAD_DIM
        y_off = jnp.dot(c_g, h_ref[:, g * gw:(g + 1) * gw].astype(BF16), preferred_element_type=F32)
        for pr in range(SSD_GROUP_HEADS // 2):
            head = g * SSD_GROUP_HEADS + 2 * pr
            ca = direction * SSD_HEADS + head
            cb = ca + 1
            c0 = head * SSD_HEAD_DIM
            xs_pair = xs_ref[0, :, c0:c0 + pair_w]
            rhs = jnp.concatenate([jnp.where(low, xs_pair, 0.0), jnp.where(low, 0.0, xs_pair)], axis=0).astype(BF16)

            def head_lhs(c):
                seg = a_cs[:, c:c + 1] - a_cs_t[c:c + 1, :]
                m = scores * jnp.exp(jnp.where(causal, seg, neg)) * dt_t[c:c + 1, :]
                return m, b_t * w_end_t[c:c + 1, :]

            m_a, bw_a = head_lhs(ca)
            m_b, bw_b = head_lhs(cb)
            lhs = jnp.concatenate([jnp.concatenate([m_a, m_b], axis=1),
                                   jnp.concatenate([bw_a, bw_b], axis=1)], axis=0).astype(BF16)
            res = jnp.dot(lhs, rhs, preferred_element_type=F32)
            e_pair = jnp.where(low, e_in[:, ca:ca + 1], e_in[:, cb:cb + 1])
            y_pair = res[:TT] + y_off[:, 2 * pr * SSD_HEAD_DIM:2 * pr * SSD_HEAD_DIM + pair_w] * e_pair
            et_pair = jnp.where(low[0:1], e_tot[:, ca:ca + 1], e_tot[:, cb:cb + 1])
            h_ref[:, c0:c0 + pair_w] = h_ref[:, c0:c0 + pair_w] * et_pair + res[TT:]
            if final:
                y_ref[:, c0:c0 + pair_w] = yin_ref[0, :, c0:c0 + pair_w] + y_pair
            else:
                o_ref[0, :, c0:c0 + pair_w] = xs_pair * dsk_ref[:, c0:c0 + pair_w] + y_pair

    if final:
        u = y_ref[...] * _silu(z_ref[0])
        o_ref[0] = (u * lax.rsqrt(jnp.mean(u * u, axis=-1, keepdims=True) + LN_EPS) * nw_ref[...]).astype(o_ref.dtype)


def _ssd_scan(xs_c, bc_c, dt_sp, a_log, direction, nct, nt, *, d_skip=None, y_in=None, proj=None, norm_w=None):
    nb, tt, _ = xs_c.shape
    nbc = bc_c.shape[-1]
    final = y_in is not None
    tile_of = _scan_tile_index(nct, nt, direction == 1)
    tok = lambda w, cb=0: pl.BlockSpec((1, TT, w), lambda b, i: (b, tile_of(i), cb))
    full = lambda shape: pl.BlockSpec(shape, lambda b, i: (0,) * len(shape))
    aneg = jnp.zeros((1, LANE), F32).at[0, :2 * SSD_HEADS].set(-jnp.exp(a_log.astype(F32)).reshape(-1))
    ones = np.ones((TT, TT), np.float32)
    tri = jnp.asarray(np.tril(ones) if direction == 0 else np.triu(ones))
    args = [xs_c, bc_c, dt_sp, aneg, tri]
    in_specs = [tok(D), tok(nbc), tok(LANE), full((1, LANE)), full((TT, TT))]
    scratch = [pltpu.VMEM((SSD_STATE, D), F32)]
    if final:
        args += [y_in, proj, norm_w[None, :]]
        in_specs += [tok(D), tok(D, AB_Z // D), full((1, D))]
        scratch.append(pltpu.VMEM((TT, D), F32))
        out_dtype = BF16
    else:
        args.append(jnp.repeat(d_skip.astype(F32), SSD_HEAD_DIM)[None, :])
        in_specs.append(full((1, D)))
        out_dtype = F32
    return pl.pallas_call(
        functools.partial(_ssd_scan_kernel, direction=direction, final=final),
        grid=(nb, nt), in_specs=in_specs, out_specs=tok(D),
        out_shape=jax.ShapeDtypeStruct((nb, tt, D), out_dtype), scratch_shapes=scratch,
        compiler_params=_cparams("parallel", "arbitrary"), name=f"ssd_scan{direction}")(*args)


def _rope_head_order():
    quarter = HEAD_DIM // 4
    return np.concatenate([np.arange(quarter) + o * quarter for o in (0, 2, 1, 3)])


def _rope(x, cos, sin):
    return x * cos + pltpu.roll(x, HEAD_DIM // 2, 1) * sin


def _attn_kernel(sink_ref, q_ref, kp_ref, kc_ref, kn_ref, vp_ref, vc_ref, vn_ref, kx_ref, vx_ref,
                 cp_ref, cc_ref, cn_ref, sp_ref, sc_ref, sn_ref, o_ref, *, nct, nt):
    i = pl.program_id(1)
    nt_dims = (((1,), (1,)), ((), ()))
    rows = ATTN_GROUP * TT
    neg = jnp.float32(-jnp.inf)

    def scores(q, k):
        return lax.dot_general(q, k.astype(BF16), nt_dims, preferred_element_type=F32) * ATTN_SCALE

    def pv(p, v):
        return jnp.dot(p.astype(BF16), v.astype(BF16), preferred_element_type=F32)

    def head_slice(ref, g):
        return ref[0, :, g * HEAD_DIM:(g + 1) * HEAD_DIM]

    def sink_col(g):
        return jnp.concatenate(
            [jnp.full((TT, 1), sink_ref[g * ATTN_GROUP + j], F32) for j in range(ATTN_GROUP)], axis=0)

    def q_rows(g, rope):
        parts = [q_ref[0, :, (g * ATTN_GROUP + j) * HEAD_DIM:(g * ATTN_GROUP + j + 1) * HEAD_DIM]
                 for j in range(ATTN_GROUP)]
        return jnp.concatenate([rope(p) for p in parts], axis=0).astype(BF16)

    def store(g, o, den):
        o = o / den
        for j in range(ATTN_GROUP):
            c0 = (g * ATTN_GROUP + j) * HEAD_DIM
            o_ref[0, :, c0:c0 + HEAD_DIM] = o[j * TT:(j + 1) * TT].astype(o_ref.dtype)

    rmax = lambda s: jnp.max(s, axis=-1, keepdims=True)
    rsum = lambda p: jnp.sum(p, axis=-1, keepdims=True)

    @pl.when(i >= nct)
    def _():
        cos_c, sin_c = cc_ref[...], sc_ref[...]
        row = lax.broadcasted_iota(jnp.int32, (rows, TT), 0) & (TT - 1)
        col = lax.broadcasted_iota(jnp.int32, (rows, TT), 1)
        keep_prev = col >= row + jnp.where(i > nct, 0, TT)
        keep_next = col <= row - jnp.where(i < nt - 1, 0, TT)
        for g in range(ATTN_KV):
            q = q_rows(g, lambda x: _rope(x, cos_c, sin_c))
            s_p = jnp.where(keep_prev, scores(q, _rope(head_slice(kp_ref, g), cp_ref[...], sp_ref[...])), neg)
            s_c = scores(q, _rope(head_slice(kc_ref, g), cos_c, sin_c))
            s_n = jnp.where(keep_next, scores(q, _rope(head_slice(kn_ref, g), cn_ref[...], sn_ref[...])), neg)
            s_x = scores(q, head_slice(kx_ref, g))
            snk = sink_col(g)
            halves = lambda t: [t[:, c0:c0 + TT] for c0 in range(0, t.shape[1], TT)]
            m_el = functools.reduce(jnp.maximum, [s_p, s_c, s_n] + halves(s_x))
            m = jnp.maximum(rmax(m_el), snk)
            p_p, p_c, p_n, p_x = jnp.exp(s_p - m), jnp.exp(s_c - m), jnp.exp(s_n - m), jnp.exp(s_x - m)
            den = rsum(functools.reduce(jnp.add, [p_p, p_c, p_n] + halves(p_x))) + jnp.exp(snk - m)
            o = (pv(p_p, head_slice(vp_ref, g)) + pv(p_c, head_slice(vc_ref, g)) + pv(p_n, head_slice(vn_ref, g))
                 + pv(p_x, head_slice(vx_ref, g)))
            store(g, o, den)

    @pl.when(i < nct)
    def _():
        for g in range(ATTN_KV):
            q = q_rows(g, lambda x: x)
            s_x = scores(q, head_slice(kx_ref, g))
            snk = sink_col(g)
            halves = lambda t: [t[:, c0:c0 + TT] for c0 in range(0, t.shape[1], TT)]
            m = jnp.maximum(rmax(functools.reduce(jnp.maximum, halves(s_x))), snk)
            p_x = jnp.exp(s_x - m)
            store(g, pv(p_x, head_slice(vx_ref, g)), rsum(functools.reduce(jnp.add, halves(p_x))) + jnp.exp(snk - m))


def _rope_tables(n_ctx, s):
    quarter = HEAD_DIM // 4
    inv_freq = ROPE_BASE ** (-jnp.arange(quarter, dtype=F32) / quarter)
    pos = jnp.arange(s)
    ang_r = (pos // GRID_W).astype(F32)[:, None] * inv_freq[None, :]
    ang_c = (pos % GRID_W).astype(F32)[:, None] * inv_freq[None, :]
    cos = jnp.concatenate([jnp.cos(ang_r), jnp.cos(ang_c), jnp.cos(ang_r), jnp.cos(ang_c)], axis=-1)
    sin = jnp.concatenate([-jnp.sin(ang_r), -jnp.sin(ang_c), jnp.sin(ang_r), jnp.sin(ang_c)], axis=-1)
    pad = jnp.zeros((n_ctx, HEAD_DIM), F32)
    return jnp.concatenate([pad, cos], axis=0), jnp.concatenate([pad, sin], axis=0)


def _attention(proj, sink, cos, sin, n_ctx, nct, nt):
    nb, tt, _ = proj.shape
    kvw = ATTN_KV * HEAD_DIM
    prev_i = lambda i: jnp.maximum(i - 1, 0)
    next_i = lambda i: jnp.minimum(i + 1, nt - 1)

    def kv(base, which):
        return pl.BlockSpec((1, TT, kvw), lambda b, i: (b, which(i), base // kvw))

    def tab(which):
        return pl.BlockSpec((TT, HEAD_DIM), lambda b, i: (which(i), 0))

    same = lambda i: i
    in_specs = [pl.BlockSpec(memory_space=pltpu.SMEM),
                pl.BlockSpec((1, TT, D), lambda b, i: (b, i, AB_Q // D)),
                kv(AB_K, prev_i), kv(AB_K, same), kv(AB_K, next_i),
                kv(AB_V, prev_i), kv(AB_V, same), kv(AB_V, next_i),
                pl.BlockSpec((1, n_ctx, kvw), lambda b, i: (b, 0, AB_K // kvw)),
                pl.BlockSpec((1, n_ctx, kvw), lambda b, i: (b, 0, AB_V // kvw)),
                tab(prev_i), tab(same), tab(next_i), tab(prev_i), tab(same), tab(next_i)]
    return pl.pallas_call(
        functools.partial(_attn_kernel, nct=nct, nt=nt),
        grid=(nb, nt), in_specs=in_specs,
        out_specs=pl.BlockSpec((1, TT, D), lambda b, i: (b, i, 0)),
        out_shape=jax.ShapeDtypeStruct((nb, tt, D), BF16),
        compiler_params=_cparams("parallel", "arbitrary"), name="attention",
    )(sink.astype(F32), proj, proj, proj, proj, proj, proj, proj, proj, proj, cos, cos, cos, sin, sin, sin)


def _conformer_kernel(a_ref, ap_ref, an_ref, b_ref, bp_ref, bn_ref, w_ref, cb_ref, ln_ref, o_ref, ext_ref, acc_ref,
                      *, nct, nt):
    i = pl.program_id(1)
    has_prev, has_next = _segment_edges(i, nct, nt)
    _fill_ext(ext_ref, a_ref[0] * _sigmoid(b_ref[0]), ap_ref[0] * _sigmoid(bp_ref[0]), an_ref[0] * _sigmoid(bn_ref[0]),
              has_prev, has_next)

    def cols(c, carry):
        c0 = pl.multiple_of(c * LANE, LANE)
        acc_ref[:, pl.ds(c0, LANE)] = _dwconv_cols_grouped(ext_ref, w_ref, cb_ref, CONF_KERNEL, CONF_PAD_L, c0, LANE)
        return carry

    lax.fori_loop(0, D // LANE, cols, 0)
    u = acc_ref[...]
    mu = jnp.mean(u, axis=-1, keepdims=True)
    uc = u - mu
    var = jnp.mean(uc * uc, axis=-1, keepdims=True)
    o_ref[0] = _silu(uc * lax.rsqrt(var + LN_EPS) * ln_ref[0:1, :] + ln_ref[1:2, :]).astype(o_ref.dtype)


def _conformer(proj, dw_w, dw_b, ln_g, ln_b, nct, nt):
    nb, tt, _ = proj.shape
    full = lambda shape: pl.BlockSpec(shape, lambda b, i: (0,) * len(shape))
    return pl.pallas_call(
        functools.partial(_conformer_kernel, nct=nct, nt=nt),
        grid=(nb, nt),
        in_specs=[*_tile_specs(D, CD_GLU_A // D, nt), *_tile_specs(D, CD_GLU_B // D, nt),
                  full((CONF_KERNEL, D)), full((1, D)), full((2, D))],
        out_specs=pl.BlockSpec((1, TT, D), lambda b, i: (b, i, 0)),
        out_shape=jax.ShapeDtypeStruct((nb, tt, D), BF16),
        scratch_shapes=[pltpu.VMEM((TT + 2 * HALO, D), F32), pltpu.VMEM((TT, D), F32)],
        compiler_params=_cparams("parallel", "arbitrary"), name="conformer",
    )(proj, proj, proj, proj, proj, proj, dw_w, dw_b[None, :], jnp.stack([ln_g, ln_b]))


def _rg_kernel(*refs, direction, final, nct, nt):
    if final:
        (x_ref, xp_ref, xn_ref, w_ref, cb_ref, gw_ref, gb_ref, lam_ref, hf_ref, gate_ref, o_ref,
         ext_ref, a_ref, b_ref, h_ref, hs_ref) = refs
    else:
        x_ref, xp_ref, xn_ref, w_ref, cb_ref, gw_ref, gb_ref, lam_ref, o_ref, ext_ref, a_ref, b_ref, h_ref, hs_ref = refs
    reverse = direction == 1
    i = pl.program_id(1)

    @pl.when(i == 0)
    def _():
        h_ref[...] = jnp.zeros_like(h_ref)

    has_prev, has_next = _segment_edges(_scan_tile_index(nct, nt, reverse)(i), nct, nt)
    _fill_ext(ext_ref, x_ref[0], xp_ref[0], xn_ref[0], has_prev, has_next)
    for blk in range(RG_BLOCKS):
        c0 = blk * RG_BLOCK
        xr = _dwconv_cols(ext_ref, w_ref, cb_ref, SHORT_CONV, SHORT_PAD_L, c0, RG_BLOCK)
        pre = jnp.dot(xr.astype(BF16), gw_ref[blk], preferred_element_type=F32) + gb_ref[blk:blk + 1, :]
        r = _sigmoid(pre[:, :RG_BLOCK])
        gate_i = _sigmoid(pre[:, RG_BLOCK:])
        log_a = -RG_C * r * _softplus(-lam_ref[:, c0:c0 + RG_BLOCK])
        a = jnp.exp(log_a)
        a_ref[:, c0:c0 + RG_BLOCK] = a
        b_ref[:, c0:c0 + RG_BLOCK] = jnp.sqrt(1.0 - a * a) * gate_i * xr

    def step(s, h):
        t = (TT - 1 - s) if reverse else s
        h = a_ref[pl.ds(t, 1), :] * h + b_ref[pl.ds(t, 1), :]
        hs_ref[pl.ds(t, 1), :] = h
        return h

    h_ref[...] = lax.fori_loop(0, TT, step, h_ref[...], unroll=SUBLANE)
    if final:
        o_ref[0] = ((hf_ref[0] + hs_ref[...]) * _gelu_tanh(gate_ref[0])).astype(o_ref.dtype)
    else:
        o_ref[0] = hs_ref[...]


def _rg_direction(proj, conv_w, conv_b, gate_w, gate_b, lam, direction, nct, nt, *, h_fwd=None):
    nb, tt, _ = proj.shape
    final = h_fwd is not None
    tile_of = _scan_tile_index(nct, nt, direction == 1)
    full = lambda shape: pl.BlockSpec(shape, lambda b, i: (0,) * len(shape))
    tok = lambda cb=0: pl.BlockSpec((1, TT, D), lambda bb, i: (bb, tile_of(i), cb))
    gw = jnp.transpose(gate_w[direction], (1, 2, 0, 3)).reshape(RG_BLOCKS, RG_BLOCK, 2 * RG_BLOCK).astype(BF16)
    gb = jnp.transpose(gate_b[direction].reshape(2, RG_BLOCKS, RG_BLOCK), (1, 0, 2)).reshape(RG_BLOCKS, 2 * RG_BLOCK)
    args = [proj, proj, proj, conv_w, conv_b[None, :], gw, gb.astype(F32), lam[direction][None, :].astype(F32)]
    in_specs = [*_tile_specs(D, CD_RX // D, nt, tile_of), full((SHORT_CONV, D)), full((1, D)),
                full((RG_BLOCKS, RG_BLOCK, 2 * RG_BLOCK)), full((RG_BLOCKS, 2 * RG_BLOCK)), full((1, D))]
    if final:
        args += [h_fwd, proj]
        in_specs += [tok(), tok(CD_RGATE // D)]
    return pl.pallas_call(
        functools.partial(_rg_kernel, direction=direction, final=final, nct=nct, nt=nt),
        grid=(nb, nt), in_specs=in_specs, out_specs=tok(),
        out_shape=jax.ShapeDtypeStruct((nb, tt, D), BF16 if final else F32),
        scratch_shapes=[pltpu.VMEM((TT + 2 * HALO, D), F32), pltpu.VMEM((TT, D), F32), pltpu.VMEM((TT, D), F32),
                        pltpu.VMEM((1, D), F32), pltpu.VMEM((TT, D), F32)],
        compiler_params=_cparams("parallel", "arbitrary"), name=f"rg_dir{direction}")(*args)


def _even_mixer(h, w_in, w_out, conv_w, conv_b, dt_bias, a_log, d_skip, norm_w, sink, cos, sin, n_ctx, nct, nt):
    nb, tt, _ = h.shape
    w = w_in
    def rope_order(cols):
        heads = cols.shape[1] // HEAD_DIM
        return cols.reshape(D, heads, HEAD_DIM)[:, :, _rope_head_order()].reshape(D, heads * HEAD_DIM)

    w_perm = jnp.concatenate([w[:, 2048:4096], w[:, 0:2048], rope_order(w[:, 5184:7232]), w[:, 4096:5120],
                              rope_order(w[:, 7232:7744]), w[:, 7744:8256], w[:, 5120:5184],
                              jnp.zeros((D, AB_N - 8256), w.dtype)], axis=1)
    proj = _matmul([h.reshape(nb * tt, D)], [w_perm.astype(BF16)], F32, "ab_in").reshape(nb, tt, AB_N)
    xs_c, bc_c, dt_sp = _ssd_prep(proj, conv_w, conv_b, dt_bias, nct, nt)
    y_fwd = _ssd_scan(xs_c, bc_c, dt_sp, a_log, 0, nct, nt, d_skip=d_skip)
    ssd_o = _ssd_scan(xs_c, bc_c, dt_sp, a_log, 1, nct, nt, y_in=y_fwd, proj=proj, norm_w=norm_w)
    att_o = _attention(proj, sink, cos, sin, n_ctx, nct, nt)
    wo = w_out.astype(BF16)
    return _matmul([ssd_o.reshape(nb * tt, D), att_o.reshape(nb * tt, D)], [wo[:D], wo[D:]], BF16, "ab_out")


def _odd_mixer(h, w_in, w_out, dw_w, dw_b, cln_g, cln_b, rconv_w, rconv_b, gate_w, gate_b, lam, nct, nt):
    nb, tt, _ = h.shape
    proj = _matmul([h.reshape(nb * tt, D)], [w_in.astype(BF16)], F32, "cd_in").reshape(nb, tt, CD_N)
    conf_o = _conformer(proj, dw_w, dw_b, cln_g, cln_b, nct, nt)
    h_fwd = _rg_direction(proj, rconv_w, rconv_b, gate_w, gate_b, lam, 0, nct, nt)
    rg_o = _rg_direction(proj, rconv_w, rconv_b, gate_w, gate_b, lam, 1, nct, nt, h_fwd=h_fwd)
    wo = w_out.astype(BF16)
    return _matmul([conf_o.reshape(nb * tt, D), rg_o.reshape(nb * tt, D)], [wo[:D], wo[D:]], BF16, "cd_out")


def kernel(x, c, ctx, c_ctx, mod_w, mod_b, ln_g, ln_b, ab_w_in, ab_w_out, ssd_conv_w, ssd_conv_b, ssd_dt_bias, ssd_a_log, ssd_d, ssd_norm_w, attn_sink, ffn_w1, ffn_w3, ffn_w2, cd_w_in, cd_w_out, conf_dw_w, conf_dw_b, conf_ln_g, conf_ln_b, rg_conv_w, rg_conv_b, rg_gate_w, rg_gate_b, rg_lambda, moe_router, moe_w1, moe_w3, moe_w2):
    nb, s, _ = x.shape
    n_ctx = ctx.shape[1]
    depth = mod_w.shape[0]
    assert s % TT == 0 and n_ctx % TT == 0 and n_ctx >= TT and x.shape[2] == D
    tt = n_ctx + s
    nct, nt = n_ctx // TT, tt // TT
    alpha = (2 * depth) ** 0.25

    modt = _modulation_tables(c, c_ctx, mod_w, mod_b)
    cos, sin = _rope_tables(n_ctx, s)
    xs = jnp.concatenate([ctx, x], axis=1)
    (h,) = _post(xs, n_ctx, nxt=(modt[0], 0, 1))
    for l in range(depth):
        j = l // 2
        if l % 2 == 0:
            mix = _even_mixer(h, ab_w_in[j], ab_w_out[j], ssd_conv_w[j], ssd_conv_b[j], ssd_dt_bias[j], ssd_a_log[j],
                              ssd_d[j], ssd_norm_w[j], attn_sink[j], cos, sin, n_ctx, nct, nt)
        else:
            mix = _odd_mixer(h, cd_w_in[j], cd_w_out[j], conf_dw_w[j], conf_dw_b[j], conf_ln_g[j], conf_ln_b[j],
                             rg_conv_w[j], rg_conv_b[j], rg_gate_w[j], rg_gate_b[j], rg_lambda[j], nct, nt)
        ln0 = jnp.stack([ln_g[l, 0], ln_b[l, 0]])
        ln1 = jnp.stack([ln_g[l, 1], ln_b[l, 1]])
        res = (mix.reshape(nb, tt, D), modt[l], 2, ln0, alpha)
        hf = h.reshape(nb * tt, D)
        if l % 2 == 0:
            xs, h = _post(xs, n_ctx, res=res, nxt=(modt[l], 3, 4))
            hf = h.reshape(nb * tt, D)
            act = _swiglu_up(hf, ffn_w1[j].astype(BF16), ffn_w3[j].astype(BF16), "ffn_up")
            f = _matmul([act], [ffn_w2[j].astype(BF16)], BF16, "ffn_down")
        else:
            wr = jnp.zeros((D, LANE), F32).at[:, :N_EXPERTS].set(moe_router[j])
            xs, h, comb = _post(xs, n_ctx, res=res, nxt=(modt[l], 3, 4), router=wr)
            hf = h.reshape(nb * tt, D)
            f = _moe_sparse(hf, comb.reshape(nb * tt, LANE), moe_w1, moe_w3, moe_w2, j)
        res = (f.reshape(nb, tt, D), modt[l], 5, ln1, alpha)
        if l + 1 < depth:
            xs, h = _post(xs, n_ctx, res=res, nxt=(modt[l + 1], 0, 1))
        else:
            (out,) = _post(xs, n_ctx, res=res, latent_only=True)
    return out
```
